```python
import jax, jax.numpy as jnp
from jax import lax
import numpy as np

D_MODEL = 4096
BATCH = 4
SEQ = 2048
DEPTH = 2

CHUNK = 64
N_META = 16
N_MIXERS = 2
N_RWKV_LAYERS = (DEPTH + 1) // 2
N_SB_LAYERS = DEPTH // 2
RWKV_HEAD = 64
RWKV_HEADS = D_MODEL // RWKV_HEAD
N_MU = 6
D_DECAY_LORA = 128
D_AAA_LORA = 128
D_GATE_LORA = 480
SB_HEAD = 128
SB_HEADS = D_MODEL // SB_HEAD
Q_BLOCK = 128
D_FF = 4 * D_MODEL
LN_EPS = 1e-5
GN_EPS = 64e-5
L2_EPS = 1e-12
DEEPNORM_ALPHA = (2 * DEPTH) ** 0.25
DEEPNORM_BETA = (8 * DEPTH) ** -0.25

kernel_name = "hybrid_rwkv7_stickbreaking_deepnorm"


def layer_norm(x, g, b):
    xf = x.astype(jnp.float32)
    mu = jnp.mean(xf, axis=-1, keepdims=True)
    var = jnp.mean(jnp.square(xf - mu), axis=-1, keepdims=True)
    return ((xf - mu) * lax.rsqrt(var + LN_EPS)).astype(x.dtype) * g + b


def token_shift(x):
    return jnp.pad(x[:, :-1], ((0, 0), (1, 0), (0, 0)))


def rwkv7_time_mix(x, mu, w_rkv, w0, w1, w2, a0, a1, a2, g1, g2,
                   k_k, k_a, r_k, gn_g, gn_b, w_o):
    B, T, D = x.shape
    H, N = RWKV_HEADS, RWKV_HEAD
    f32 = jnp.float32
    xx = token_shift(x) - x
    mixed = x[None] + xx[None] * mu[:, None, None, :]
    rkv = jnp.einsum('nbtd,nde->nbte', mixed[:3], w_rkv)
    r, k, v = rkv[0], rkv[1], rkv[2]
    xw, xa, xg = mixed[3], mixed[4], mixed[5]

    w_log = -jax.nn.softplus(-(w0 + jnp.tanh(xw @ w1) @ w2)) - 0.5
    decay = jnp.exp(-jnp.exp(w_log.astype(f32)))
    a = jax.nn.sigmoid(a0 + (xa @ a1) @ a2)
    g = jax.nn.sigmoid(xg @ g1) @ g2

    def heads(z):
        return z.reshape(B, T, H, N).astype(f32)

    kk = heads(k * k_k)
    kk = kk / jnp.maximum(jnp.sqrt(jnp.sum(jnp.square(kk), axis=-1, keepdims=True)), L2_EPS)
    k = k * (1.0 + (a - 1.0) * k_a)
    r_h, k_h, v_h, a_h, w_h = heads(r), heads(k), heads(v), heads(a), decay.reshape(B, T, H, N)

    def step(S, inp):
        r_t, w_t, k_t, v_t, kk_t, a_t = inp
        s_kk = jnp.einsum('bhij,bhj->bhi', S, kk_t)
        S = (S * w_t[:, :, None, :]
             - jnp.einsum('bhi,bhj->bhij', s_kk, kk_t * a_t)
             + jnp.einsum('bhi,bhj->bhij', v_t, k_t))
        return S, jnp.einsum('bhij,bhj->bhi', S, r_t)

    xs = tuple(jnp.moveaxis(z, 1, 0) for z in (r_h, w_h, k_h, v_h, kk, a_h))
    S0 = jnp.zeros((B, H, N, N), f32)
    _, ys = lax.scan(step, S0, xs)
    y = jnp.moveaxis(ys, 0, 1)

    mu_y = jnp.mean(y, axis=-1, keepdims=True)
    var_y = jnp.mean(jnp.square(y - mu_y), axis=-1, keepdims=True)
    y_n = ((y - mu_y) * lax.rsqrt(var_y + GN_EPS)).reshape(B, T, D) * gn_g + gn_b
    bonus = (jnp.sum(r_h * k_h * r_k, axis=-1, keepdims=True) * v_h).reshape(B, T, D)
    return ((y_n + bonus).astype(x.dtype) * g) @ w_o


def stick_breaking_attention(x, w_qkv, w_o):
    B, T, D = x.shape
    H, Dh = SB_HEADS, SB_HEAD
    qkv = (x @ w_qkv).reshape(B, T, 3, H, Dh)
    q, k, v = qkv[:, :, 0], qkv[:, :, 1], qkv[:, :, 2]
    scale = Dh ** -0.5
    n_blocks = -(-T // Q_BLOCK)
    outs = []
    for blk in range(n_blocks):
        t0 = blk * Q_BLOCK
        t1 = min(T, t0 + Q_BLOCK)
        kb, vb = k[:, :t1], v[:, :t1]
        z = jnp.einsum('bqhd,bkhd->bhqk', q[:, t0:t1], kb).astype(jnp.float32) * scale
        causal = jnp.arange(t1)[None, :] < jnp.arange(t0, t1)[:, None]
        log_keep = jnp.where(causal, jax.nn.log_sigmoid(-z), 0.0)
        suffix = lax.cumsum(log_keep, axis=3, reverse=True) - log_keep
        attn = jnp.where(causal, jnp.exp(jax.nn.log_sigmoid(z) + suffix), 0.0)
        outs.append(jnp.einsum('bhqk,bkhd->bqhd', attn.astype(v.dtype), vb))
    o = jnp.concatenate(outs, axis=1).reshape(B, T, D)
    return o @ w_o


def squared_relu_mlp(x, w_up, w_down):
    return jnp.square(jax.nn.relu(x @ w_up)) @ w_down


def setup_inputs(seed: int = 0) -> dict:
    key = jax.random.key(seed)
    ks = jax.random.split(key, 32)
    D, R, S = D_MODEL, N_RWKV_LAYERS, N_SB_LAYERS
    nrm = jax.random.normal
    uni = jax.random.uniform
    f32 = jnp.float32
    return {
        'x': nrm(ks[0], (BATCH, SEQ, D), f32),
        'meta_tokens': nrm(ks[1], (N_META, D), f32),
        'ln_mix_g': 1.0 + 0.02 * nrm(ks[2], (DEPTH, D), f32),
        'ln_mix_b': 0.02 * nrm(ks[3], (DEPTH, D), f32),
        'ln_ffn_g': 1.0 + 0.02 * nrm(ks[4], (DEPTH, D), f32),
        'ln_ffn_b': 0.02 * nrm(ks[5], (DEPTH, D), f32),
        'w_up': nrm(ks[6], (DEPTH, D, D_FF), f32) * D ** -0.5,
        'w_down': nrm(ks[7], (DEPTH, D_FF, D), f32) * (DEEPNORM_BETA * D_FF ** -0.5),
        'rwkv_mu': uni(ks[8], (R, N_MU, D), f32),
        'rwkv_w_rkv': nrm(ks[9], (R, 3, D, D), f32) * D ** -0.5,
        'rwkv_w0': -6.0 + 5.0 * uni(ks[10], (R, D), f32),
        'rwkv_w1': nrm(ks[11], (R, D, D_DECAY_LORA), f32) * D ** -0.5,
        'rwkv_w2': nrm(ks[12], (R, D_DECAY_LORA, D), f32) * (0.5 * D_DECAY_LORA ** -0.5),
        'rwkv_a0': 0.1 * nrm(ks[13], (R, D), f32),
        'rwkv_a1': nrm(ks[14], (R, D, D_AAA_LORA), f32) * D ** -0.5,
        'rwkv_a2': nrm(ks[15], (R, D_AAA_LORA, D), f32) * (0.5 * D_AAA_LORA ** -0.5),
        'rwkv_g1': nrm(ks[16], (R, D, D_GATE_LORA), f32) * D ** -0.5,
        'rwkv_g2': nrm(ks[17], (R, D_GATE_LORA, D), f32) * D_GATE_LORA ** -0.5,
        'rwkv_k_k': 0.85 + 0.05 * nrm(ks[18], (R, D), f32),
        'rwkv_k_a': 1.0 + 0.05 * nrm(ks[19], (R, D), f32),
        'rwkv_r_k': 0.1 * nrm(ks[20], (R, RWKV_HEADS, RWKV_HEAD), f32),
        'rwkv_gn_g': 1.0 + 0.02 * nrm(ks[21], (R, D), f32),
        'rwkv_gn_b': 0.02 * nrm(ks[22], (R, D), f32),
        'rwkv_w_o': nrm(ks[23], (R, D, D), f32) * (DEEPNORM_BETA * D ** -0.5),
        'sb_w_qkv': nrm(ks[24], (S, D, 3 * D), f32) * D ** -0.5,
        'sb_w_o': nrm(ks[25], (S, D, D), f32) * (DEEPNORM_BETA * D ** -0.5),
    }


def reference(x, meta_tokens, ln_mix_g, ln_mix_b, ln_ffn_g, ln_ffn_b, w_up, w_down,
              rwkv_mu, rwkv_w_rkv, rwkv_w0, rwkv_w1, rwkv_w2, rwkv_a0, rwkv_a1, rwkv_a2,
              rwkv_g1, rwkv_g2, rwkv_k_k, rwkv_k_a, rwkv_r_k, rwkv_gn_g, rwkv_gn_b, rwkv_w_o,
              sb_w_qkv, sb_w_o):
    B = x.shape[0]
    meta = jnp.broadcast_to(meta_tokens[None].astype(x.dtype), (B, N_META, D_MODEL))
    h = jnp.concatenate([meta, x], axis=1)
    for i in range(DEPTH):
        j = i // N_MIXERS
        if i % N_MIXERS == 0:
            mix = rwkv7_time_mix(h, rwkv_mu[j], rwkv_w_rkv[j], rwkv_w0[j], rwkv_w1[j], rwkv_w2[j],
                                 rwkv_a0[j], rwkv_a1[j], rwkv_a2[j], rwkv_g1[j], rwkv_g2[j],
                                 rwkv_k_k[j], rwkv_k_a[j], rwkv_r_k[j], rwkv_gn_g[j], rwkv_gn_b[j],
                                 rwkv_w_o[j])
        else:
            mix = stick_breaking_attention(h, sb_w_qkv[j], sb_w_o[j])
        h = layer_norm(DEEPNORM_ALPHA * h + mix, ln_mix_g[i], ln_mix_b[i])
        h = layer_norm(DEEPNORM_ALPHA * h + squared_relu_mlp(h, w_up[i], w_down[i]),
                       ln_ffn_g[i], ln_ffn_b[i])
    return h[:, N_META:]
```

```python
import functools

import jax
import jax.numpy as jnp
from jax import lax
from jax.experimental import pallas as pl
from jax.experimental.pallas import tpu as pltpu

F32 = jnp.float32
BF16 = jnp.bfloat16

N_META = 16
RWKV_HEAD = 64
SB_HEAD = 128
LN_EPS = 1e-5
GN_EPS = 64e-5
L2_EPS = 1e-12
DEPTH = 2
DEEPNORM_ALPHA = (2 * DEPTH) ** 0.25

LANES = 128
CHUNK = 64
SEQ_ALIGN = 128
VMEM_LIMIT = 56 * 1024 * 1024


def _cparams(sem):
    return pltpu.CompilerParams(dimension_semantics=sem, vmem_limit_bytes=VMEM_LIMIT)


def _pick(n, prefs):
    for p in prefs:
        if n % p == 0:
            return p
    return n


def _mm_kernel(a_ref, w_ref, o_ref, *scratch, nk, act):
    part = jnp.dot(a_ref[...], w_ref[...].astype(BF16), preferred_element_type=F32)

    def finish(acc):
        if act == "relu2":
            acc = jnp.square(jnp.maximum(acc, 0.0))
        o_ref[...] = acc.astype(o_ref.dtype)

    if nk == 1:
        finish(part)
    else:
        acc_ref, = scratch
        k = pl.program_id(2)

        @pl.when(k == 0)
        def _():
            acc_ref[...] = part

        @pl.when(k > 0)
        def _():
            acc_ref[...] += part

        @pl.when(k == nk - 1)
        def _():
            finish(acc_ref[...])


def _matmul(a, w, *, a_prefix=(), w_prefix=(), act="none", out_dtype=F32,
            tm=None, tn=None, tk=None):
    M, K = a.shape[-2:]
    N = w.shape[-1]
    tm = _pick(M, (tm or 1088, 1024, 512, 256, 128, 64, 32, 16))
    tn = _pick(N, (tn or 512, 512, 256, 128))
    tk = _pick(K, (tk or 4096, 2048, 1024, 512, 256, 128))
    nk = K // tk
    na, nw = len(a_prefix), len(w_prefix)
    a_spec = pl.BlockSpec((None,) * na + (tm, tk), lambda i, j, k: a_prefix + (i, k))
    w_spec = pl.BlockSpec((None,) * nw + (tk, tn), lambda i, j, k: w_prefix + (k, j))
    scratch = [] if nk == 1 else [pltpu.VMEM((tm, tn), F32)]
    return pl.pallas_call(
        functools.partial(_mm_kernel, nk=nk, act=act),
        out_shape=jax.ShapeDtypeStruct((M, N), out_dtype),
        grid=(M // tm, N // tn, nk),
        in_specs=[a_spec, w_spec],
        out_specs=pl.BlockSpec((tm, tn), lambda i, j, k: (i, j)),
        scratch_shapes=scratch,
        compiler_params=_cparams(("parallel", "parallel", "arbitrary")),
        name="matmul_" + act,
    )(a, w)


def _lora_kernel(a_ref, w1_ref, w2_ref, o_ref, *, act):
    t = jnp.dot(a_ref[...], w1_ref[...], preferred_element_type=F32)
    if act == "tanh":
        t = jnp.tanh(t)
    elif act == "sigmoid":
        t = 1.0 / (1.0 + jnp.exp(-t))
    o_ref[...] = jnp.dot(t.astype(BF16), w2_ref[...], preferred_element_type=F32)


def _lora(a, w1, w2, *, a_prefix, act):
    M, K = a.shape[-2:]
    R = w1.shape[-1]
    N = w2.shape[-1]
    tm = _pick(M, (544, 512, 256, 128, 64, 32, 16))
    na = len(a_prefix)
    return pl.pallas_call(
        functools.partial(_lora_kernel, act=act),
        out_shape=jax.ShapeDtypeStruct((M, N), F32),
        grid=(M // tm,),
        in_specs=[
            pl.BlockSpec((None,) * na + (tm, K), lambda i: a_prefix + (i, 0)),
            pl.BlockSpec((K, R), lambda i: (0, 0)),
            pl.BlockSpec((R, N), lambda i: (0, 0)),
        ],
        out_specs=pl.BlockSpec((tm, N), lambda i: (i, 0)),
        compiler_params=_cparams(("parallel",)),
        name="lora_" + act,
    )(a, w1, w2)


def _mix_kernel(x_ref, xp_ref, mu_ref, o_ref):
    x = x_ref[0]
    tt = x.shape[0]
    first = pl.program_id(1) == 0
    prev_row = jnp.where(first, 0.0, xp_ref[0, 7:8, :])
    rolled = pltpu.roll(x, 1, 0)
    row = lax.broadcasted_iota(jnp.int32, (tt, 1), 0)
    xprev = jnp.where(row == 0, prev_row, rolled)
    xx = xprev - x
    for n in range(o_ref.shape[0]):
        o_ref[n, 0] = (x + xx * mu_ref[n:n + 1, :]).astype(o_ref.dtype)


def _rwkv_mix(h, mu):
    B, T, D = h.shape
    n_mu = mu.shape[0]
    tt = _pick(T, (272, 256, 128, 64, 32, 16, 8))
    sub = tt // 8
    return pl.pallas_call(
        _mix_kernel,
        out_shape=jax.ShapeDtypeStruct((n_mu, B, T, D), BF16),
        grid=(B, T // tt),
        in_specs=[
            pl.BlockSpec((1, tt, D), lambda b, i: (b, i, 0)),
            pl.BlockSpec((1, 8, D), lambda b, i: (b, jnp.maximum(i * sub - 1, 0), 0)),
            pl.BlockSpec((n_mu, D), lambda b, i: (0, 0)),
        ],
        out_specs=pl.BlockSpec((n_mu, 1, tt, D), lambda b, i: (0, b, i, 0)),
        compiler_params=_cparams(("parallel", "parallel")),
        name="rwkv_mix",
    )(h, h, mu)


def _ln_kernel(h_ref, y_ref, g_ref, b_ref, o_ref, ob_ref):
    z = DEEPNORM_ALPHA * h_ref[...] + y_ref[...]
    mu = jnp.mean(z, axis=-1, keepdims=True)
    zc = z - mu
    var = jnp.mean(jnp.square(zc), axis=-1, keepdims=True)
    out = zc * lax.rsqrt(var + LN_EPS) * g_ref[...] + b_ref[...]
    o_ref[...] = out
    ob_ref[...] = out.astype(BF16)


def _add_ln(h, y, g, b):
    M, D = h.shape
    tm = _pick(M, (272, 256, 128, 64, 32, 16, 8))
    row = pl.BlockSpec((tm, D), lambda i: (i, 0))
    vec = pl.BlockSpec((1, D), lambda i: (0, 0))
    return pl.pallas_call(
        _ln_kernel,
        out_shape=(jax.ShapeDtypeStruct((M, D), F32), jax.ShapeDtypeStruct((M, D), BF16)),
        grid=(M // tm,),
        in_specs=[row, row, vec, vec],
        out_specs=(row, row),
        compiler_params=_cparams(("parallel",)),
        name="add_layernorm",
    )(h, y, g.reshape(1, D), b.reshape(1, D))


def _split3(x):
    hi = x.astype(BF16)
    r1 = x - hi.astype(F32)
    mid = r1.astype(BF16)
    lo = (r1 - mid.astype(F32)).astype(BF16)
    return hi, mid, lo


def _split2(x):
    hi = x.astype(BF16)
    lo = (x - hi.astype(F32)).astype(BF16)
    return hi, lo


def _dot(a, b):
    return jnp.dot(a, b, preferred_element_type=F32)


def _dot_nt(a, b):
    return lax.dot_general(a, b, (((1,), (1,)), ((), ())), preferred_element_type=F32)


def _dot_tn(a, b):
    return lax.dot_general(a, b, (((0,), (0,)), ((), ())), preferred_element_type=F32)


def _scan_kernel(r_ref, k_ref, v_ref, wl_ref, al_ref, g_ref,
                 w0_ref, a0_ref, kk_ref, ka_ref, rk_ref, gg_ref, gb_ref,
                 o_ref, s_ref):
    L = CHUNK
    hb = r_ref.shape[-1]
    n_pairs = hb // LANES
    P = 2 * L

    @pl.when(pl.program_id(2) == 0)
    def _():
        s_ref[...] = jnp.zeros_like(s_ref)

    r = r_ref[0]
    k = k_ref[0]
    v = v_ref[0]

    wpre = w0_ref[...] + wl_ref[0]
    nsp = jnp.maximum(-wpre, 0.0) + jnp.log1p(jnp.exp(-jnp.abs(wpre)))
    lw = -jnp.exp(-nsp - 0.5)
    a = 1.0 / (1.0 + jnp.exp(-(a0_ref[...] + al_ref[0])))

    li = lax.broadcasted_iota(jnp.int32, (2 * LANES, LANES), 0)
    lj = lax.broadcasted_iota(jnp.int32, (2 * LANES, LANES), 1)
    seg_ones = jnp.where(((li % LANES) // RWKV_HEAD) == (lj // RWKV_HEAD), 1.0, 0.0).astype(BF16)
    ti = lax.broadcasted_iota(jnp.int32, (L, L), 0)
    tj = lax.broadcasted_iota(jnp.int32, (L, L), 1)
    tril_incl = jnp.where(tj <= ti, 1.0, 0.0).astype(BF16)

    def segsum(x):
        hi, lo = _split2(x)
        return _dot(jnp.concatenate([hi, lo], axis=1), seg_ones)

    parts = _split3(lw)
    cs = _dot(tril_incl, jnp.concatenate(parts, axis=1))
    G = cs[:, :hb] + cs[:, hb:2 * hb] + cs[:, 2 * hb:]
    GL = G[L - 1:L, :]
    eG = jnp.exp(G)
    eGp = jnp.exp(G - lw)
    enG = jnp.exp(-G)
    eGL = jnp.exp(GL)
    eTail = jnp.exp(GL - G)

    pi = lax.broadcasted_iota(jnp.int32, (P, P), 0)
    pj = lax.broadcasted_iota(jnp.int32, (P, P), 1)
    strict = (pj % L) < (pi % L)
    incl = (pj % L) <= (pi % L)
    eye = jnp.where(pi == pj, 1.0, 0.0)
    lane = lax.broadcasted_iota(jnp.int32, (L, LANES), 1)
    head0 = lane < RWKV_HEAD

    def stack(x):
        return jnp.concatenate([jnp.where(head0, x, 0.0), jnp.where(head0, 0.0, x)], axis=0)

    for p in range(n_pairs):
        sl = slice(p * LANES, (p + 1) * LANES)
        rp, kp, vp, ap = r[:, sl], k[:, sl], v[:, sl], a[:, sl]
        kkr = kp * kk_ref[:, sl]
        nrm = jnp.sqrt(segsum(kkr * kkr))
        kk = kkr / jnp.maximum(nrm, L2_EPS)
        k2 = kp * (1.0 + (ap - 1.0) * ka_ref[:, sl])
        b = kk * ap

        Rs = stack(rp * eG[:, sl]).astype(BF16)
        Qs = stack(kk * eGp[:, sl]).astype(BF16)
        Kt = stack(k2 * enG[:, sl]).astype(BF16)
        Bt = stack(b * enG[:, sl]).astype(BF16)
        Kb = stack(k2 * eTail[:, sl]).astype(BF16)
        Bb = stack(b * eTail[:, sl]).astype(BF16)
        Vs = stack(vp).astype(BF16)

        AA = _dot_nt(jnp.concatenate([Qs, Rs], axis=0), jnp.concatenate([Bt, Kt], axis=0))
        A_qb = jnp.where(strict, AA[:P, :P], 0.0)
        A_qk = jnp.where(strict, AA[:P, P:], 0.0)
        A_rb = jnp.where(incl, AA[P:, :P], 0.0)
        A_rk = jnp.where(incl, AA[P:, P:], 0.0)

        Npow = -A_qb
        T = eye + Npow
        n_sq = (L - 1).bit_length() - 1
        for _ in range(n_sq):
            nb = Npow.astype(BF16)
            Npow = _dot(nb, nb)
            T = T + _dot(T.astype(BF16), Npow.astype(BF16))

        AV = _dot(A_qk.astype(BF16), Vs)
        WU = _dot(T.astype(BF16), jnp.concatenate([Qs, AV.astype(BF16)], axis=1))
        W = WU[:, :LANES]
        Uv = WU[:, LANES:]

        S = s_ref[p]
        WR = _dot_nt(jnp.concatenate([W.astype(BF16), Rs], axis=0), S.astype(BF16))
        U = WR[:P] + Uv
        Ub = U.astype(BF16)
        Ys = WR[P:] + _dot(jnp.concatenate([A_rk.astype(BF16), (-A_rb).astype(BF16)], axis=1),
                           jnp.concatenate([Vs, Ub], axis=0))
        s_ref[p] = S * eGL[:, sl] + _dot_tn(jnp.concatenate([Vs, -Ub], axis=0),
                                            jnp.concatenate([Kb, Bb], axis=0))
        y = Ys[:L] + Ys[L:]

        mu = segsum(y) * (1.0 / RWKV_HEAD)
        yc = y - mu
        var = segsum(yc * yc) * (1.0 / RWKV_HEAD)
        y_n = yc * lax.rsqrt(var + GN_EPS) * gg_ref[:, sl] + gb_ref[:, sl]
        bonus = segsum(rp * k2 * rk_ref[:, sl]) * vp
        o_ref[0, :, sl] = ((y_n + bonus) * g_ref[0, :, sl]).astype(o_ref.dtype)


def _rwkv_scan(r, k, v, wl, al, g, w0, a0, k_k, k_a, r_k, gn_g, gn_b):
    B, T, D = r.shape
    hb = _pick(D, (512, 256, 128))
    seq = pl.BlockSpec((1, CHUNK, hb), lambda b, h, c: (b, c, h))
    vec = pl.BlockSpec((1, hb), lambda b, h, c: (0, h))
    vecs = [x.reshape(1, D) for x in (w0, a0, k_k, k_a, r_k, gn_g, gn_b)]
    return pl.pallas_call(
        _scan_kernel,
        out_shape=jax.ShapeDtypeStruct((B, T, D), BF16),
        grid=(B, D // hb, T // CHUNK),
        in_specs=[seq] * 6 + [vec] * 7,
        out_specs=seq,
        scratch_shapes=[pltpu.VMEM((hb // LANES, LANES, LANES), F32)],
        compiler_params=_cparams(("parallel", "parallel", "arbitrary")),
        name="rwkv7_scan",
    )(r, k, v, wl, al, g, *vecs)


def _sb_kernel(q_ref, k_ref, v_ref, cm_ref, o_ref):
    tq = q_ref.shape[1]
    tk = tq
    qi = pl.program_id(2)
    scale = SB_HEAD ** -0.5
    q = q_ref[0]
    cm = cm_ref[...]

    def block(kb, carry, masked):
        c, acc = carry
        ks = k_ref[0, pl.ds(pl.multiple_of(kb * tk, tk), tk), :]
        vs = v_ref[0, pl.ds(pl.multiple_of(kb * tk, tk), tk), :]
        z = _dot_nt(q, ks) * scale
        sp = jnp.maximum(z, 0.0) + jnp.log1p(jnp.exp(-jnp.abs(z)))
        lk = -sp
        if masked:
            ri = lax.broadcasted_iota(jnp.int32, (tq, tk), 0)
            ci = lax.broadcasted_iota(jnp.int32, (tq, tk), 1)
            causal = ci < ri
            lk = jnp.where(causal, lk, 0.0)
        hi, lo = _split2(lk)
        sums = _dot(jnp.concatenate([hi, lo], axis=1), cm)
        suffix = sums[:, :tk] - lk
        logit = z - sp + suffix + c
        pexp = jnp.exp(logit)
        if masked:
            pexp = jnp.where(causal, pexp, 0.0)
        acc = acc + _dot(pexp.astype(BF16), vs)
        return c + sums[:, tk:], acc

    zero = jnp.zeros((tq, tk), F32)
    carry = block(qi, (zero, jnp.zeros((tq, SB_HEAD), F32)), True)
    carry = lax.fori_loop(0, qi, lambda n, cr: block(qi - 1 - n, cr, False), carry)
    o_ref[0] = carry[1].astype(o_ref.dtype)


def _sb_attention(qkv, n_heads):
    B, T, _ = qkv.shape
    tq = SB_HEAD
    ri = jnp.arange(2 * tq)[:, None] % tq
    ci = jnp.arange(2 * tq)[None, :]
    cm = jnp.where(ci < tq, ri >= ci, True).astype(BF16)
    return pl.pallas_call(
        _sb_kernel,
        out_shape=jax.ShapeDtypeStruct((B, T, n_heads * SB_HEAD), BF16),
        grid=(B, n_heads, T // tq),
        in_specs=[
            pl.BlockSpec((1, tq, SB_HEAD), lambda b, h, i: (b, i, h)),
            pl.BlockSpec((1, T, SB_HEAD), lambda b, h, i: (b, 0, n_heads + h)),
            pl.BlockSpec((1, T, SB_HEAD), lambda b, h, i: (b, 0, 2 * n_heads + h)),
            pl.BlockSpec((2 * tq, 2 * tq), lambda b, h, i: (0, 0)),
        ],
        out_specs=pl.BlockSpec((1, tq, SB_HEAD), lambda b, h, i: (b, i, h)),
        compiler_params=_cparams(("parallel", "parallel", "arbitrary")),
        name="stick_breaking_attention",
    )(qkv, qkv, qkv, cm)


def kernel(x, meta_tokens, ln_mix_g, ln_mix_b, ln_ffn_g, ln_ffn_b, w_up, w_down, rwkv_mu, rwkv_w_rkv, rwkv_w0, rwkv_w1, rwkv_w2, rwkv_a0, rwkv_a1, rwkv_a2, rwkv_g1, rwkv_g2, rwkv_k_k, rwkv_k_a, rwkv_r_k, rwkv_gn_g, rwkv_gn_b, rwkv_w_o, sb_w_qkv, sb_w_o):
    B, seq_len, D = x.shape
    depth = w_up.shape[0]
    T = N_META + seq_len
    Tp = -(-T // SEQ_ALIGN) * SEQ_ALIGN
    M = B * Tp
    meta = jnp.broadcast_to(meta_tokens[None].astype(x.dtype), (B, N_META, D))
    h = jnp.concatenate([meta, x, jnp.zeros((B, Tp - T, D), x.dtype)], axis=1).reshape(M, D)
    hb = None
    for i in range(depth):
        j = i // 2
        if i % 2 == 0:
            mixed = _rwkv_mix(h.reshape(B, Tp, D), rwkv_mu[j]).reshape(-1, M, D)
            r, k, v = (_matmul(mixed, rwkv_w_rkv, a_prefix=(n,), w_prefix=(j, n)) for n in range(3))
            wl = _lora(mixed, rwkv_w1[j].astype(BF16), rwkv_w2[j].astype(BF16), a_prefix=(3,), act="tanh")
            al = _lora(mixed, rwkv_a1[j].astype(BF16), rwkv_a2[j].astype(BF16), a_prefix=(4,), act="none")
            g = _lora(mixed, rwkv_g1[j].astype(BF16), rwkv_g2[j].astype(BF16), a_prefix=(5,), act="sigmoid")
            r, k, v, wl, al, g = (z.reshape(B, Tp, D) for z in (r, k, v, wl, al, g))
            y = _rwkv_scan(r, k, v, wl, al, g, rwkv_w0[j], rwkv_a0[j], rwkv_k_k[j], rwkv_k_a[j],
                           rwkv_r_k[j], rwkv_gn_g[j], rwkv_gn_b[j])
            mix = _matmul(y.reshape(M, D), rwkv_w_o, w_prefix=(j,))
        else:
            qkv = _matmul(hb, sb_w_qkv, w_prefix=(j,), out_dtype=BF16)
            o = _sb_attention(qkv.reshape(B, Tp, 3 * D), D // SB_HEAD)
            mix = _matmul(o.reshape(M, D), sb_w_o, w_prefix=(j,))
        h, hb = _add_ln(h, mix, ln_mix_g[i], ln_mix_b[i])
        up = _matmul(hb, w_up, w_prefix=(i,), act="relu2", out_dtype=BF16)
        down = _matmul(up, w_down, w_prefix=(i,), tk=2048, tn=1024)
        h, hb = _add_ln(h, down, ln_ffn_g[i], ln_ffn_b[i])
    return h.reshape(B, Tp, D)[:, N_META:T]
```

```python
import functools

import jax
import jax.numpy as jnp
from jax import lax
from jax.experimental import pallas as pl
from jax.experimental.pallas import tpu as pltpu

F32 = jnp.float32
BF16 = jnp.bfloat16

N_META = 16
RWKV_HEAD = 64
SB_HEAD = 128
LN_EPS = 1e-5
GN_EPS = 64e-5
L2_EPS = 1e-12
DEPTH = 2
DEEPNORM_ALPHA = (2 * DEPTH) ** 0.25

LANES = 128
CHUNK = 64
SEQ_ALIGN = 128
VMEM_LIMIT = 56 * 1024 * 1024


def _cparams(sem):
    return pltpu.CompilerParams(dimension_semantics=sem, vmem_limit_bytes=VMEM_LIMIT)


def _pick(n, prefs):
    for p in prefs:
        if n % p == 0:
            return p
    return n


def _mm_kernel(a_ref, w_ref, o_ref, *scratch, nk, act):
    part = jnp.dot(a_ref[...], w_ref[...].astype(BF16), preferred_element_type=F32)

    def finish(acc):
        if act == "relu2":
            acc = jnp.square(jnp.maximum(acc, 0.0))
        o_ref[...] = acc.astype(o_ref.dtype)

    if nk == 1:
        finish(part)
    else:
        acc_ref, = scratch
        k = pl.program_id(2)

        @pl.when(k == 0)
        def _():
            acc_ref[...] = part

        @pl.when(k > 0)
        def _():
            acc_ref[...] += part

        @pl.when(k == nk - 1)
        def _():
            finish(acc_ref[...])


def _matmul(a, w, *, a_prefix=(), w_prefix=(), act="none", out_dtype=F32,
            tm=None, tn=None, tk=None):
    M, K = a.shape[-2:]
    N = w.shape[-1]
    tm = _pick(M, (tm or 1088, 1024, 512, 256, 128, 64, 32, 16))
    tn = _pick(N, (tn or 512, 512, 256, 128))
    tk = _pick(K, (tk or 4096, 2048, 1024, 512, 256, 128))
    nk = K // tk
    na, nw = len(a_prefix), len(w_prefix)
    a_spec = pl.BlockSpec((None,) * na + (tm, tk), lambda i, j, k: a_prefix + (i, k))
    w_spec = pl.BlockSpec((None,) * nw + (tk, tn), lambda i, j, k: w_prefix + (k, j))
    scratch = [] if nk == 1 else [pltpu.VMEM((tm, tn), F32)]
    return pl.pallas_call(
        functools.partial(_mm_kernel, nk=nk, act=act),
        out_shape=jax.ShapeDtypeStruct((M, N), out_dtype),
        grid=(M // tm, N // tn, nk),
        in_specs=[a_spec, w_spec],
        out_specs=pl.BlockSpec((tm, tn), lambda i, j, k: (i, j)),
        scratch_shapes=scratch,
        compiler_params=_cparams(("parallel", "parallel", "arbitrary")),
        name="matmul_" + act,
    )(a, w)


def _lora_kernel(a_ref, w1_ref, w2_ref, o_ref, *, act):
    t = jnp.dot(a_ref[...], w1_ref[...], preferred_element_type=F32)
    if act == "tanh":
        t = jnp.tanh(t)
    elif act == "sigmoid":
        t = 1.0 / (1.0 + jnp.exp(-t))
    o_ref[...] = jnp.dot(t.astype(BF16), w2_ref[...], preferred_element_type=F32)


def _lora(a, w1, w2, *, a_prefix, act):
    M, K = a.shape[-2:]
    R = w1.shape[-1]
    N = w2.shape[-1]
    tm = _pick(M, (544, 512, 256, 128, 64, 32, 16))
    na = len(a_prefix)
    return pl.pallas_call(
        functools.partial(_lora_kernel, act=act),
        out_shape=jax.ShapeDtypeStruct((M, N), F32),
        grid=(M // tm,),
        in_specs=[
            pl.BlockSpec((None,) * na + (tm, K), lambda i: a_prefix + (i, 0)),
            pl.BlockSpec((K, R), lambda i: (0, 0)),
            pl.BlockSpec((R, N), lambda i: (0, 0)),
        ],
        out_specs=pl.BlockSpec((tm, N), lambda i: (i, 0)),
        compiler_params=_cparams(("parallel",)),
        name="lora_" + act,
    )(a, w1, w2)


def _mix_kernel(x_ref, xp_ref, mu_ref, o_ref):
    x = x_ref[0]
    tt = x.shape[0]
    first = pl.program_id(1) == 0
    prev_row = jnp.where(first, 0.0, xp_ref[0, 7:8, :])
    rolled = pltpu.roll(x, 1, 0)
    row = lax.broadcasted_iota(jnp.int32, (tt, 1), 0)
    xprev = jnp.where(row == 0, prev_row, rolled)
    xx = xprev - x
    for n in range(o_ref.shape[0]):
        o_ref[n, 0] = (x + xx * mu_ref[n:n + 1, :]).astype(o_ref.dtype)


def _rwkv_mix(h, mu):
    B, T, D = h.shape
    n_mu = mu.shape[0]
    tt = _pick(T, (272, 256, 128, 64, 32, 16, 8))
    sub = tt // 8
    return pl.pallas_call(
        _mix_kernel,
        out_shape=jax.ShapeDtypeStruct((n_mu, B, T, D), BF16),
        grid=(B, T // tt),
        in_specs=[
            pl.BlockSpec((1, tt, D), lambda b, i: (b, i, 0)),
            pl.BlockSpec((1, 8, D), lambda b, i: (b, jnp.maximum(i * sub - 1, 0), 0)),
            pl.BlockSpec((n_mu, D), lambda b, i: (0, 0)),
        ],
        out_specs=pl.BlockSpec((n_mu, 1, tt, D), lambda b, i: (0, b, i, 0)),
        compiler_params=_cparams(("parallel", "parallel")),
        name="rwkv_mix",
    )(h, h, mu)


def _ln_kernel(h_ref, y_ref, g_ref, b_ref, o_ref, ob_ref):
    z = DEEPNORM_ALPHA * h_ref[...] + y_ref[...]
    mu = jnp.mean(z, axis=-1, keepdims=True)
    zc = z - mu
    var = jnp.mean(jnp.square(zc), axis=-1, keepdims=True)
    out = zc * lax.rsqrt(var + LN_EPS) * g_ref[...] + b_ref[...]
    o_ref[...] = out
    ob_ref[...] = out.astype(BF16)


def _add_ln(h, y, g, b):
    M, D = h.shape
    tm = _pick(M, (272, 256, 128, 64, 32, 16, 8))
    row = pl.BlockSpec((tm, D), lambda i: (i, 0))
    vec = pl.BlockSpec((1, D), lambda i: (0, 0))
    return pl.pallas_call(
        _ln_kernel,
        out_shape=(jax.ShapeDtypeStruct((M, D), F32), jax.ShapeDtypeStruct((M, D), BF16)),
        grid=(M // tm,),
        in_specs=[row, row, vec, vec],
        out_specs=(row, row),
        compiler_params=_cparams(("parallel",)),
        name="add_layernorm",
    )(h, y, g.reshape(1, D), b.reshape(1, D))


def _split3(x):
    hi = x.astype(BF16)
    r1 = x - hi.astype(F32)
    mid = r1.astype(BF16)
    lo = (r1 - mid.astype(F32)).astype(BF16)
    return hi, mid, lo


def _split2(x):
    hi = x.astype(BF16)
    lo = (x - hi.astype(F32)).astype(BF16)
    return hi, lo


def _dot(a, b):
    return jnp.dot(a, b, preferred_element_type=F32)


def _dot_nt(a, b):
    return lax.dot_general(a, b, (((1,), (1,)), ((), ())), preferred_element_type=F32)


def _dot_tn(a, b):
    return lax.dot_general(a, b, (((0,), (0,)), ((), ())), preferred_element_type=F32)


def _scan_kernel(r_ref, k_ref, v_ref, wl_ref, al_ref, g_ref,
                 w0_ref, a0_ref, kk_ref, ka_ref, rk_ref, gg_ref, gb_ref,
                 o_ref, s_ref):
    L = CHUNK
    hb = r_ref.shape[-1]
    n_pairs = hb // LANES
    P = 2 * L

    @pl.when(pl.program_id(2) == 0)
    def _():
        s_ref[...] = jnp.zeros_like(s_ref)

    r = r_ref[0]
    k = k_ref[0]
    v = v_ref[0]

    wpre = w0_ref[...] + wl_ref[0]
    nsp = jnp.maximum(-wpre, 0.0) + jnp.log1p(jnp.exp(-jnp.abs(wpre)))
    lw = -jnp.exp(-nsp - 0.5)
    a = 1.0 / (1.0 + jnp.exp(-(a0_ref[...] + al_ref[0])))

    li = lax.broadcasted_iota(jnp.int32, (2 * LANES, LANES), 0)
    lj = lax.broadcasted_iota(jnp.int32, (2 * LANES, LANES), 1)
    seg_ones = jnp.where(((li % LANES) // RWKV_HEAD) == (lj // RWKV_HEAD), 1.0, 0.0).astype(BF16)
    ti = lax.broadcasted_iota(jnp.int32, (L, L), 0)
    tj = lax.broadcasted_iota(jnp.int32, (L, L), 1)
    tril_incl = jnp.where(tj <= ti, 1.0, 0.0).astype(BF16)

    def to_rows(x):
        return jnp.concatenate([x[:, p * LANES:(p + 1) * LANES] for p in range(n_pairs)], axis=0)

    def to_lanes(x):
        return jnp.concatenate([x[p * L:(p + 1) * L] for p in range(n_pairs)], axis=1)

    def segsum(x):
        hi, lo = _split2(to_rows(x))
        return to_lanes(_dot(jnp.concatenate([hi, lo], axis=1), seg_ones))

    parts = _split3(lw)
    cs = _dot(tril_incl, jnp.concatenate(parts, axis=1))
    G = cs[:, :hb] + cs[:, hb:2 * hb] + cs[:, 2 * hb:]
    GL = G[L - 1:L, :]
    enG = jnp.exp(-G)
    eGL = jnp.exp(GL)
    eTail = jnp.exp(GL - G)

    kkr = k * kk_ref[...]
    kk = kkr / jnp.maximum(jnp.sqrt(segsum(kkr * kkr)), L2_EPS)
    k2 = k * (1.0 + (a - 1.0) * ka_ref[...])
    b = kk * a
    full = dict(R=r * jnp.exp(G), Q=kk * jnp.exp(G - lw), Kt=k2 * enG, Bt=b * enG,
                Kb=k2 * eTail, Bb=b * eTail, V=v)

    pi = lax.broadcasted_iota(jnp.int32, (P, P), 0)
    pj = lax.broadcasted_iota(jnp.int32, (P, P), 1)
    strict = (pj % L) < (pi % L)
    incl = (pj % L) <= (pi % L)
    eye = jnp.where(pi == pj, 1.0, 0.0)
    lane = lax.broadcasted_iota(jnp.int32, (L, LANES), 1)
    head0 = lane < RWKV_HEAD

    def stack(x):
        return jnp.concatenate([jnp.where(head0, x, 0.0), jnp.where(head0, 0.0, x)], axis=0).astype(BF16)

    pairs = range(n_pairs)
    st = {name: [stack(x[:, p * LANES:(p + 1) * LANES]) for p in pairs] for name, x in full.items()}
    S = [s_ref[p] for p in pairs]

    AA = [_dot_nt(jnp.concatenate([st["Q"][p], st["R"][p]], axis=0),
                  jnp.concatenate([st["Bt"][p], st["Kt"][p]], axis=0)) for p in pairs]
    A_qk = [jnp.where(strict, AA[p][:P, P:], 0.0).astype(BF16) for p in pairs]
    A_r = [jnp.concatenate([jnp.where(incl, AA[p][P:, P:], 0.0).astype(BF16),
                            jnp.where(incl, -AA[p][P:, :P], 0.0).astype(BF16)], axis=1) for p in pairs]
    AV = [_dot(A_qk[p], st["V"][p]) for p in pairs]

    Npow = [jnp.where(strict, -AA[p][:P, :P], 0.0) for p in pairs]
    T = [eye + Npow[p] for p in pairs]
    for _ in range((L - 1).bit_length() - 1):
        nb = [Npow[p].astype(BF16) for p in pairs]
        Npow = [_dot(nb[p], nb[p]) for p in pairs]
        T = [T[p] + _dot(T[p].astype(BF16), Npow[p].astype(BF16)) for p in pairs]

    WU = [_dot(T[p].astype(BF16), jnp.concatenate([st["Q"][p], AV[p].astype(BF16)], axis=1)) for p in pairs]
    WR = [_dot_nt(jnp.concatenate([WU[p][:, :LANES].astype(BF16), st["R"][p]], axis=0), S[p].astype(BF16))
          for p in pairs]
    Ub = [(WR[p][:P] + WU[p][:, LANES:]).astype(BF16) for p in pairs]
    Ys = [WR[p][P:] + _dot(A_r[p], jnp.concatenate([st["V"][p], Ub[p]], axis=0)) for p in pairs]
    for p in pairs:
        s_ref[p] = S[p] * eGL[:, p * LANES:(p + 1) * LANES] + _dot_tn(
            jnp.concatenate([st["V"][p], -Ub[p]], axis=0), jnp.concatenate([st["Kb"][p], st["Bb"][p]], axis=0))
    y = jnp.concatenate([Ys[p][:L] + Ys[p][L:] for p in pairs], axis=1)

    mu = segsum(y) * (1.0 / RWKV_HEAD)
    yc = y - mu
    var = segsum(yc * yc) * (1.0 / RWKV_HEAD)
    y_n = yc * lax.rsqrt(var + GN_EPS) * gg_ref[...] + gb_ref[...]
    bonus = segsum(r * k2 * rk_ref[...]) * v
    o_ref[0] = ((y_n + bonus) * g_ref[0]).astype(o_ref.dtype)


def _rwkv_scan(r, k, v, wl, al, g, w0, a0, k_k, k_a, r_k, gn_g, gn_b):
    B, T, D = r.shape
    hb = _pick(D, (1024, 512, 256, 128))
    seq =pl.BlockSpec((1, CHUNK, hb), lambda b, h, c: (b, c, h))
    vec = pl.BlockSpec((1, hb), lambda b, h, c: (0, h))
    vecs = [x.reshape(1, D) for x in (w0, a0, k_k, k_a, r_k, gn_g, gn_b)]
    return pl.pallas_call(
        _scan_kernel,
        out_shape=jax.ShapeDtypeStruct((B, T, D), BF16),
        grid=(B, D // hb, T // CHUNK),
        in_specs=[seq] * 6 + [vec] * 7,
        out_specs=seq,
        scratch_shapes=[pltpu.VMEM((hb // LANES, LANES, LANES), F32)],
        compiler_params=_cparams(("parallel", "parallel", "arbitrary")),
        name="rwkv7_scan",
    )(r, k, v, wl, al, g, *vecs)


def _sb_kernel(q_ref, k_ref, v_ref, cm_ref, o_ref):
    tq = q_ref.shape[1]
    tk = tq
    n_h = q_ref.shape[2] // SB_HEAD
    heads = range(n_h)
    qi = pl.program_id(2)
    scale = SB_HEAD ** -0.5
    q = [q_ref[0, :, g * SB_HEAD:(g + 1) * SB_HEAD] for g in heads]
    cm = cm_ref[...]

    def block(kb, carry, masked):
        rows = pl.ds(pl.multiple_of(kb * tk, tk), tk)
        z, sp, lk = [], [], []
        for g in heads:
            ks = k_ref[0, rows, g * SB_HEAD:(g + 1) * SB_HEAD]
            zg = _dot_nt(q[g], ks) * scale
            spg = jnp.maximum(zg, 0.0) + jnp.log(1.0 + jnp.exp(-jnp.abs(zg)))
            lkg = -spg
            if masked:
                ri = lax.broadcasted_iota(jnp.int32, (tq, tk), 0)
                ci = lax.broadcasted_iota(jnp.int32, (tq, tk), 1)
                causal = ci < ri
                lkg = jnp.where(causal, lkg, 0.0)
            z.append(zg), sp.append(spg), lk.append(lkg)
        hi, lo = _split2(jnp.concatenate(lk, axis=0))
        sums = _dot(jnp.concatenate([hi, lo], axis=1), cm)
        out = []
        for g in heads:
            c, acc = carry[g]
            sg = sums[g * tq:(g + 1) * tq]
            suffix = sg[:, :tk] - lk[g]
            pexp = jnp.exp(z[g] - sp[g] + suffix + c)
            if masked:
                pexp = jnp.where(causal, pexp, 0.0)
            vs = v_ref[0, rows, g * SB_HEAD:(g + 1) * SB_HEAD]
            out.append((c + sg[:, tk:], acc + _dot(pexp.astype(BF16), vs)))
        return tuple(out)

    zero = jnp.zeros((tq, tk), F32)
    carry = block(qi, tuple((zero, jnp.zeros((tq, SB_HEAD), F32)) for _ in heads), True)
    carry = lax.fori_loop(0, qi, lambda n, cr: block(qi - 1 - n, cr, False), carry)
    o_ref[0] = jnp.concatenate([carry[g][1] for g in heads], axis=1).astype(o_ref.dtype)


def _sb_attention(qkv, n_heads):
    B, T, _ = qkv.shape
    tq = SB_HEAD
    gh = _pick(n_heads, (8, 4, 2, 1))
    ng = n_heads // gh
    ri = jnp.arange(2 * tq)[:, None] % tq
    ci = jnp.arange(2 * tq)[None, :]
    cm = jnp.where(ci < tq, ri >= ci, True).astype(BF16)
    return pl.pallas_call(
        _sb_kernel,
        out_shape=jax.ShapeDtypeStruct((B, T, n_heads * SB_HEAD), BF16),
        grid=(B, ng, T // tq),
        in_specs=[
            pl.BlockSpec((1, tq, gh * SB_HEAD), lambda b, h, i: (b, i, h)),
            pl.BlockSpec((1, T, gh * SB_HEAD), lambda b, h, i: (b, 0, ng + h)),
            pl.BlockSpec((1, T, gh * SB_HEAD), lambda b, h, i: (b, 0, 2 * ng + h)),
            pl.BlockSpec((2 * tq, 2 * tq), lambda b, h, i: (0, 0)),
        ],
        out_specs=pl.BlockSpec((1, tq, gh * SB_HEAD), lambda b, h, i: (b, i, h)),
        compiler_params=_cparams(("parallel", "parallel", "arbitrary")),
        name="stick_breaking_attention",
    )(qkv, qkv, qkv, cm)


def kernel(x, meta_tokens, ln_mix_g, ln_mix_b, ln_ffn_g, ln_ffn_b, w_up, w_down, rwkv_mu, rwkv_w_rkv, rwkv_w0, rwkv_w1, rwkv_w2, rwkv_a0, rwkv_a1, rwkv_a2, rwkv_g1, rwkv_g2, rwkv_k_k, rwkv_k_a, rwkv_r_k, rwkv_gn_g, rwkv_gn_b, rwkv_w_o, sb_w_qkv, sb_w_o):
    B, seq_len, D = x.shape
    depth = w_up.shape[0]
    T = N_META + seq_len
    Tp = -(-T // SEQ_ALIGN) * SEQ_ALIGN
    M = B * Tp
    meta = jnp.broadcast_to(meta_tokens[None].astype(x.dtype), (B, N_META, D))
    h = jnp.concatenate([meta, x, jnp.zeros((B, Tp - T, D), x.dtype)], axis=1).reshape(M, D)
    hb = None
    for i in range(depth):
        j = i // 2
        if i % 2 == 0:
            mixed = _rwkv_mix(h.reshape(B, Tp, D), rwkv_mu[j]).reshape(-1, M, D)
            r, k, v = (_matmul(mixed, rwkv_w_rkv, a_prefix=(n,), w_prefix=(j, n)) for n in range(3))
            wl = _lora(mixed, rwkv_w1[j].astype(BF16), rwkv_w2[j].astype(BF16), a_prefix=(3,), act="tanh")
            al = _lora(mixed, rwkv_a1[j].astype(BF16), rwkv_a2[j].astype(BF16), a_prefix=(4,), act="none")
            g = _lora(mixed, rwkv_g1[j].astype(BF16), rwkv_g2[j].astype(BF16), a_prefix=(5,), act="sigmoid")
            r, k, v, wl, al, g = (z.reshape(B, Tp, D) for z in (r, k, v, wl, al, g))
            y = _rwkv_scan(r, k, v, wl, al, g, rwkv_w0[j], rwkv_a0[j], rwkv_k_k[j], rwkv_k_a[j],
                           rwkv_r_k[j], rwkv_gn_g[j], rwkv_gn_b[j])
            mix = _matmul(y.reshape(M, D), rwkv_w_o, w_prefix=(j,))
        else:
            qkv = _matmul(hb, sb_w_qkv, w_prefix=(j,), out_dtype=BF16)
            o = _sb_attention(qkv.reshape(B, Tp, 3 * D), D // SB_HEAD)
            mix = _matmul(o.reshape(M, D), sb_w_o, w_prefix=(j,))
        h, hb = _add_ln(h, mix, ln_mix_g[i], ln_mix_b[i])
        up = _matmul(hb, w_up, w_prefix=(i,), act="relu2", out_dtype=BF16)
        down = _matmul(up, w_down, w_prefix=(i,), tk=2048, tn=1024)
        h, hb = _add_ln(h, down, ln_ffn_g[i], ln_ffn_b[i])
    return h.reshape(B, Tp, D)[:, N_META:T]
```

```python
import functools

import jax
import jax.numpy as jnp
from jax import lax
from jax.experimental import pallas as pl
from jax.experimental.pallas import tpu as pltpu

F32 = jnp.float32
BF16 = jnp.bfloat16

N_META = 16
RWKV_HEAD = 64
SB_HEAD = 128
LN_EPS = 1e-5
GN_EPS = 64e-5
L2_EPS = 1e-12
DEPTH = 2
DEEPNORM_ALPHA = (2 * DEPTH) ** 0.25

LANES = 128
MXU_COLS = 256
CHUNK = 64
SEQ_ALIGN = 128
VMEM_LIMIT = 56 * 1024 * 1024


def _cparams(sem):
    return pltpu.CompilerParams(dimension_semantics=sem, vmem_limit_bytes=VMEM_LIMIT)


def _pick(n, prefs):
    for p in prefs:
        if n % p == 0:
            return p
    return n


def _mm_kernel(a_ref, w_ref, o_ref, *, nk, act):
    if nk == 1:
        acc = jnp.dot(a_ref[...], w_ref[...].astype(BF16), preferred_element_type=F32)
        if act == "relu2":
            acc = jnp.square(jnp.maximum(acc, 0.0))
        o_ref[...] = acc.astype(o_ref.dtype)
        return

    tn = o_ref.shape[1]
    cw = _pick(tn, (MXU_COLS, LANES))

    def chunks(first):
        a = a_ref[...]
        for c in range(tn // cw):
            cols = slice(c * cw, (c + 1) * cw)
            part = jnp.dot(a, w_ref[:, cols].astype(BF16), preferred_element_type=F32)
            o_ref[:, cols] = part if first else o_ref[:, cols] + part

    k = pl.program_id(2)
    pl.when(k == 0)(functools.partial(chunks, True))
    pl.when(k > 0)(functools.partial(chunks, False))


def _matmul(a, w, *, a_prefix=(), w_prefix=(), act="none", out_dtype=F32,
            tm=None, tn=None, tk=None):
    M, K = a.shape[-2:]
    N = w.shape[-1]
    tm = _pick(M, (tm or 1088, 1024, 512, 256, 128, 64, 32, 16))
    tn = _pick(N, (tn or 512, 512, 256, 128))
    tk = _pick(K, (tk or 4096, 2048, 1024, 512, 256, 128))
    nk = K // tk
    na, nw = len(a_prefix), len(w_prefix)
    a_spec = pl.BlockSpec((None,) * na + (tm, tk), lambda i, j, k: a_prefix + (i, k))
    w_spec = pl.BlockSpec((None,) * nw + (tk, tn), lambda i, j, k: w_prefix + (k, j))
    assert nk == 1 or (act == "none" and out_dtype == F32), "K-tiled path accumulates in the f32 output"
    return pl.pallas_call(
        functools.partial(_mm_kernel, nk=nk, act=act),
        out_shape=jax.ShapeDtypeStruct((M, N), out_dtype),
        grid=(M // tm, N // tn, nk),
        in_specs=[a_spec, w_spec],
        out_specs=pl.BlockSpec((tm, tn), lambda i, j, k: (i, j)),
        compiler_params=_cparams(("parallel", "parallel", "arbitrary")),
        name="matmul_" + act,
    )(a, w)


def _lora_kernel(a_ref, w1_ref, w2_ref, o_ref, *, act):
    t = jnp.dot(a_ref[...], w1_ref[...], preferred_element_type=F32)
    if act == "tanh":
        t = jnp.tanh(t)
    elif act == "sigmoid":
        t = 1.0 / (1.0 + jnp.exp(-t))
    o_ref[...] = jnp.dot(t.astype(BF16), w2_ref[...], preferred_element_type=F32)


def _lora(a, w1, w2, *, a_prefix, act):
    M, K = a.shape[-2:]
    R = w1.shape[-1]
    N = w2.shape[-1]
    tm = _pick(M, (544, 512, 256, 128, 64, 32, 16))
    na = len(a_prefix)
    return pl.pallas_call(
        functools.partial(_lora_kernel, act=act),
        out_shape=jax.ShapeDtypeStruct((M, N), F32),
        grid=(M // tm,),
        in_specs=[
            pl.BlockSpec((None,) * na + (tm, K), lambda i: a_prefix + (i, 0)),
            pl.BlockSpec((K, R), lambda i: (0, 0)),
            pl.BlockSpec((R, N), lambda i: (0, 0)),
        ],
        out_specs=pl.BlockSpec((tm, N), lambda i: (i, 0)),
        compiler_params=_cparams(("parallel",)),
        name="lora_" + act,
    )(a, w1, w2)


def _mix_kernel(x_ref, xp_ref, mu_ref, o_ref):
    x = x_ref[0]
    tt = x.shape[0]
    first = pl.program_id(1) == 0
    prev_row = jnp.where(first, 0.0, xp_ref[0, 7:8, :])
    rolled = pltpu.roll(x, 1, 0)
    row = lax.broadcasted_iota(jnp.int32, (tt, 1), 0)
    xprev = jnp.where(row == 0, prev_row, rolled)
    xx = xprev - x
    for n in range(o_ref.shape[0]):
        o_ref[n, 0] = (x + xx * mu_ref[n:n + 1, :]).astype(o_ref.dtype)


def _rwkv_mix(h, mu):
    B, T, D = h.shape
    n_mu = mu.shape[0]
    tt = _pick(T, (272, 256, 128, 64, 32, 16, 8))
    sub = tt // 8
    return pl.pallas_call(
        _mix_kernel,
        out_shape=jax.ShapeDtypeStruct((n_mu, B, T, D), BF16),
        grid=(B, T // tt),
        in_specs=[
            pl.BlockSpec((1, tt, D), lambda b, i: (b, i, 0)),
            pl.BlockSpec((1, 8, D), lambda b, i: (b, jnp.maximum(i * sub - 1, 0), 0)),
            pl.BlockSpec((n_mu, D), lambda b, i: (0, 0)),
        ],
        out_specs=pl.BlockSpec((n_mu, 1, tt, D), lambda b, i: (0, b, i, 0)),
        compiler_params=_cparams(("parallel", "parallel")),
        name="rwkv_mix",
    )(h, h, mu)


def _ln_kernel(h_ref, y_ref, g_ref, b_ref, o_ref, ob_ref):
    z = DEEPNORM_ALPHA * h_ref[...] + y_ref[...]
    mu = jnp.mean(z, axis=-1, keepdims=True)
    zc = z - mu
    var = jnp.mean(jnp.square(zc), axis=-1, keepdims=True)
    out = zc * lax.rsqrt(var + LN_EPS) * g_ref[...] + b_ref[...]
    o_ref[...] = out
    ob_ref[...] = out.astype(BF16)


def _add_ln(h, y, g, b):
    M, D = h.shape
    tm = _pick(M, (272, 256, 128, 64, 32, 16, 8))
    row = pl.BlockSpec((tm, D), lambda i: (i, 0))
    vec = pl.BlockSpec((1, D), lambda i: (0, 0))
    return pl.pallas_call(
        _ln_kernel,
        out_shape=(jax.ShapeDtypeStruct((M, D), F32), jax.ShapeDtypeStruct((M, D), BF16)),
        grid=(M // tm,),
        in_specs=[row, row, vec, vec],
        out_specs=(row, row),
        compiler_params=_cparams(("parallel",)),
        name="add_layernorm",
    )(h, y, g.reshape(1, D), b.reshape(1, D))


def _split3(x):
    hi = x.astype(BF16)
    r1 = x - hi.astype(F32)
    mid = r1.astype(BF16)
    lo = (r1 - mid.astype(F32)).astype(BF16)
    return hi, mid, lo


def _split2(x):
    hi = x.astype(BF16)
    lo = (x - hi.astype(F32)).astype(BF16)
    return hi, lo


def _dot(a, b):
    return jnp.dot(a, b, preferred_element_type=F32)


def _dot_nt(a, b):
    return lax.dot_general(a, b, (((1,), (1,)), ((), ())), preferred_element_type=F32)


def _dot_tn(a, b):
    return lax.dot_general(a, b, (((0,), (0,)), ((), ())), preferred_element_type=F32)


def _scan_kernel(r_ref, k_ref, v_ref, wl_ref, al_ref, g_ref,
                 w0_ref, a0_ref, kk_ref, ka_ref, rk_ref, gg_ref, gb_ref,
                 o_ref, s_ref):
    L = CHUNK
    hb = r_ref.shape[-1]
    n_pairs = hb // LANES
    P = 2 * L

    @pl.when(pl.program_id(2) == 0)
    def _():
        s_ref[...] = jnp.zeros_like(s_ref)

    r = r_ref[0]
    k = k_ref[0]
    v = v_ref[0]

    wpre = w0_ref[...] + wl_ref[0]
    nsp = jnp.maximum(-wpre, 0.0) + jnp.log1p(jnp.exp(-jnp.abs(wpre)))
    lw = -jnp.exp(-nsp - 0.5)
    a = 1.0 / (1.0 + jnp.exp(-(a0_ref[...] + al_ref[0])))

    li = lax.broadcasted_iota(jnp.int32, (2 * LANES, LANES), 0)
    lj = lax.broadcasted_iota(jnp.int32, (2 * LANES, LANES), 1)
    seg_ones = jnp.where(((li % LANES) // RWKV_HEAD) == (lj // RWKV_HEAD), 1.0, 0.0).astype(BF16)
    ti = lax.broadcasted_iota(jnp.int32, (L, L), 0)
    tj = lax.broadcasted_iota(jnp.int32, (L, L), 1)
    tril_incl = jnp.where(tj <= ti, 1.0, 0.0).astype(BF16)

    def to_rows(x):
        return jnp.concatenate([x[:, p * LANES:(p + 1) * LANES] for p in range(n_pairs)], axis=0)

    def to_lanes(x):
        return jnp.concatenate([x[p * L:(p + 1) * L] for p in range(n_pairs)], axis=1)

    def segsum(x):
        hi, lo = _split2(to_rows(x))
        return to_lanes(_dot(jnp.concatenate([hi, lo], axis=1), seg_ones))

    parts = _split3(lw)
    cs = _dot(tril_incl, jnp.concatenate(parts, axis=1))
    G = cs[:, :hb] + cs[:, hb:2 * hb] + cs[:, 2 * hb:]
    GL = G[L - 1:L, :]
    enG = jnp.exp(-G)
    eGL = jnp.exp(GL)
    eTail = jnp.exp(GL - G)

    kkr = k * kk_ref[...]
    kk = kkr / jnp.maximum(jnp.sqrt(segsum(kkr * kkr)), L2_EPS)
    k2 = k * (1.0 + (a - 1.0) * ka_ref[...])
    b = kk * a
    full = dict(R=r * jnp.exp(G), Q=kk * jnp.exp(G - lw), Kt=k2 * enG, Bt=b * enG,
                Kb=k2 * eTail, Bb=b * eTail, V=v)

    pi = lax.broadcasted_iota(jnp.int32, (P, P), 0)
    pj = lax.broadcasted_iota(jnp.int32, (P, P), 1)
    strict = (pj % L) < (pi % L)
    incl = (pj % L) <= (pi % L)
    eye = jnp.where(pi == pj, 1.0, 0.0)
    lane = lax.broadcasted_iota(jnp.int32, (L, LANES), 1)
    head0 = lane < RWKV_HEAD

    def stack(x):
        return jnp.concatenate([jnp.where(head0, x, 0.0), jnp.where(head0, 0.0, x)], axis=0).astype(BF16)

    pairs = range(n_pairs)
    st = {name: [stack(x[:, p * LANES:(p + 1) * LANES]) for p in pairs] for name, x in full.items()}
    S = [s_ref[p] for p in pairs]

    AA = [_dot_nt(jnp.concatenate([st["Q"][p], st["R"][p]], axis=0),
                  jnp.concatenate([st["Bt"][p], st["Kt"][p]], axis=0)) for p in pairs]
    A_qk = [jnp.where(strict, AA[p][:P, P:], 0.0).astype(BF16) for p in pairs]
    A_r = [jnp.concatenate([jnp.where(incl, AA[p][P:, P:], 0.0).astype(BF16),
                            jnp.where(incl, -AA[p][P:, :P], 0.0).astype(BF16)], axis=1) for p in pairs]
    AV = [_dot(A_qk[p], st["V"][p]) for p in pairs]

    Npow = [jnp.where(strict, -AA[p][:P, :P], 0.0) for p in pairs]
    T = [eye + Npow[p] for p in pairs]
    for _ in range((L - 1).bit_length() - 1):
        nb = [Npow[p].astype(BF16) for p in pairs]
        Npow = [_dot(nb[p], nb[p]) for p in pairs]
        T = [T[p] + _dot(T[p].astype(BF16), Npow[p].astype(BF16)) for p in pairs]

    WU = [_dot(T[p].astype(BF16), jnp.concatenate([st["Q"][p], AV[p].astype(BF16)], axis=1)) for p in pairs]
    WR = [_dot_nt(jnp.concatenate([WU[p][:, :LANES].astype(BF16), st["R"][p]], axis=0), S[p].astype(BF16))
          for p in pairs]
    Ub = [(WR[p][:P] + WU[p][:, LANES:]).astype(BF16) for p in pairs]
    Ys = [WR[p][P:] + _dot(A_r[p], jnp.concatenate([st["V"][p], Ub[p]], axis=0)) for p in pairs]
    for p in pairs:
        s_ref[p] = S[p] * eGL[:, p * LANES:(p + 1) * LANES] + _dot_tn(
            jnp.concatenate([st["V"][p], -Ub[p]], axis=0), jnp.concatenate([st["Kb"][p], st["Bb"][p]], axis=0))
    y = jnp.concatenate([Ys[p][:L] + Ys[p][L:] for p in pairs], axis=1)

    mu = segsum(y) * (1.0 / RWKV_HEAD)
    yc = y - mu
    var = segsum(yc * yc) * (1.0 / RWKV_HEAD)
    y_n = yc * lax.rsqrt(var + GN_EPS) * gg_ref[...] + gb_ref[...]
    bonus = segsum(r * k2 * rk_ref[...]) * v
    o_ref[0] = ((y_n + bonus) * g_ref[0]).astype(o_ref.dtype)


def _rwkv_scan(r, k, v, wl, al, g, w0, a0, k_k, k_a, r_k, gn_g, gn_b):
    B, T, D = r.shape
    hb = _pick(D, (1024, 512, 256, 128))
    seq =pl.BlockSpec((1, CHUNK, hb), lambda b, h, c: (b, c, h))
    vec = pl.BlockSpec((1, hb), lambda b, h, c: (0, h))
    vecs = [x.reshape(1, D) for x in (w0, a0, k_k, k_a, r_k, gn_g, gn_b)]
    return pl.pallas_call(
        _scan_kernel,
        out_shape=jax.ShapeDtypeStruct((B, T, D), BF16),
        grid=(B, D // hb, T // CHUNK),
        in_specs=[seq] * 6 + [vec] * 7,
        out_specs=seq,
        scratch_shapes=[pltpu.VMEM((hb // LANES, LANES, LANES), F32)],
        compiler_params=_cparams(("parallel", "parallel", "arbitrary")),
        name="rwkv7_scan",
    )(r, k, v, wl, al, g, *vecs)


def _sb_kernel(q_ref, k_ref, v_ref, cm_ref, o_ref):
    tq = q_ref.shape[1]
    tk = tq
    n_h = q_ref.shape[2] // SB_HEAD
    heads = range(n_h)
    qi = pl.program_id(2)
    scale = SB_HEAD ** -0.5
    q = [q_ref[0, :, g * SB_HEAD:(g + 1) * SB_HEAD] for g in heads]
    cm = cm_ref[...]

    def block(kb, carry, masked):
        rows = pl.ds(pl.multiple_of(kb * tk, tk), tk)
        zn, lk = [], []
        for g in heads:
            ks = k_ref[0, rows, g * SB_HEAD:(g + 1) * SB_HEAD]
            zng = _dot_nt(q[g], ks) * (-scale)
            neg_abs = lax.bitcast_convert_type(
                lax.bitcast_convert_type(zng, jnp.int32) | jnp.int32(-2 ** 31), F32)
            lkg = jnp.minimum(zng, 0.0) - jnp.log(jnp.maximum(1.0 + jnp.exp(neg_abs), 1.0))
            if masked:
                ri = lax.broadcasted_iota(jnp.int32, (tq, tk), 0)
                ci = lax.broadcasted_iota(jnp.int32, (tq, tk), 1)
                causal = ci < ri
                lkg = jnp.where(causal, lkg, 0.0)
            zn.append(zng), lk.append(lkg)
        hi, lo = _split2(jnp.concatenate(lk, axis=0))
        sums = _dot(jnp.concatenate([hi, lo], axis=1), cm)
        out = []
        for g in heads:
            c, acc = carry[g]
            sg = sums[g * tq:(g + 1) * tq]
            pexp = jnp.exp(sg[:, :tk] - zn[g] + c)
            if masked:
                pexp = jnp.where(causal, pexp, 0.0)
            vs = v_ref[0, rows, g * SB_HEAD:(g + 1) * SB_HEAD]
            out.append((c + sg[:, tk:], acc + _dot(pexp.astype(BF16), vs)))
        return tuple(out)

    zero = jnp.zeros((tq, tk), F32)
    carry = block(qi, tuple((zero, jnp.zeros((tq, SB_HEAD), F32)) for _ in heads), True)
    carry = lax.fori_loop(0, qi, lambda n, cr: block(qi - 1 - n, cr, False), carry)
    o_ref[0] = jnp.concatenate([carry[g][1] for g in heads], axis=1).astype(o_ref.dtype)


def _sb_attention(qkv, n_heads):
    B, T, _ = qkv.shape
    tq = SB_HEAD
    gh = _pick(n_heads, (8, 4, 2, 1))
    ng = n_heads // gh
    ri = jnp.arange(2 * tq)[:, None] % tq
    ci = jnp.arange(2 * tq)[None, :]
    cm = jnp.where(ci < tq, ri >= ci, True).astype(BF16)
    return pl.pallas_call(
        _sb_kernel,
        out_shape=jax.ShapeDtypeStruct((B, T, n_heads * SB_HEAD), BF16),
        grid=(B, ng, T // tq),
        in_specs=[
            pl.BlockSpec((1, tq, gh * SB_HEAD), lambda b, h, i: (b, i, h)),
            pl.BlockSpec((1, T, gh * SB_HEAD), lambda b, h, i: (b, 0, ng + h)),
            pl.BlockSpec((1, T, gh * SB_HEAD), lambda b, h, i: (b, 0, 2 * ng + h)),
            pl.BlockSpec((2 * tq, 2 * tq), lambda b, h, i: (0, 0)),
        ],
        out_specs=pl.BlockSpec((1, tq, gh * SB_HEAD), lambda b, h, i: (b, i, h)),
        compiler_params=_cparams(("parallel", "parallel", "arbitrary")),
        name="stick_breaking_attention",
    )(qkv, qkv, qkv, cm)


def kernel(x, meta_tokens, ln_mix_g, ln_mix_b, ln_ffn_g, ln_ffn_b, w_up, w_down, rwkv_mu, rwkv_w_rkv, rwkv_w0, rwkv_w1, rwkv_w2, rwkv_a0, rwkv_a1, rwkv_a2, rwkv_g1, rwkv_g2, rwkv_k_k, rwkv_k_a, rwkv_r_k, rwkv_gn_g, rwkv_gn_b, rwkv_w_o, sb_w_qkv, sb_w_o):
    B, seq_len, D = x.shape
    depth = w_up.shape[0]
    T = N_META + seq_len
    Tp = -(-T // SEQ_ALIGN) * SEQ_ALIGN
    M = B * Tp
    meta = jnp.broadcast_to(meta_tokens[None].astype(x.dtype), (B, N_META, D))
    h = jnp.concatenate([meta, x, jnp.zeros((B, Tp - T, D), x.dtype)], axis=1).reshape(M, D)
    hb = None
    for i in range(depth):
        j = i // 2
        if i % 2 == 0:
            mixed = _rwkv_mix(h.reshape(B, Tp, D), rwkv_mu[j]).reshape(-1, M, D)
            tiles = ((1088, 512), (1088, 512), (1088, 256))
            r, k, v = (_matmul(mixed, rwkv_w_rkv, a_prefix=(n,), w_prefix=(j, n), tm=tiles[n][0], tn=tiles[n][1])
                       for n in range(3))
            wl = _lora(mixed, rwkv_w1[j].astype(BF16), rwkv_w2[j].astype(BF16), a_prefix=(3,), act="tanh")
            al = _lora(mixed, rwkv_a1[j].astype(BF16), rwkv_a2[j].astype(BF16), a_prefix=(4,), act="none")
            g = _lora(mixed, rwkv_g1[j].astype(BF16), rwkv_g2[j].astype(BF16), a_prefix=(5,), act="sigmoid")
            r, k, v, wl, al, g = (z.reshape(B, Tp, D) for z in (r, k, v, wl, al, g))
            y = _rwkv_scan(r, k, v, wl, al, g, rwkv_w0[j], rwkv_a0[j], rwkv_k_k[j], rwkv_k_a[j],
                           rwkv_r_k[j], rwkv_gn_g[j], rwkv_gn_b[j])
            mix = _matmul(y.reshape(M, D), rwkv_w_o, w_prefix=(j,), tm=544, tn=1024)
        else:
            qkv = _matmul(hb, sb_w_qkv, w_prefix=(j,), out_dtype=BF16)
            o = _sb_attention(qkv.reshape(B, Tp, 3 * D), D // SB_HEAD)
            mix = _matmul(o.reshape(M, D), sb_w_o, w_prefix=(j,))
        h, hb = _add_ln(h, mix, ln_mix_g[i], ln_mix_b[i])
        up = _matmul(hb, w_up, w_prefix=(i,), act="relu2", out_dtype=BF16, tm=1088, tn=(512, 256)[i])
        down = _matmul(up, w_down, w_prefix=(i,), tk=(2048, 4096)[i], tn=(1024, 512)[i])
        h, hb = _add_ln(h, down, ln_ffn_g[i], ln_ffn_b[i])
    return h.reshape(B, Tp, D)[:, N_META:T]
```

```python
import functools

import jax
import jax.numpy as jnp
from jax import lax
from jax.experimental import pallas as pl
from jax.experimental.pallas import tpu as pltpu

F32 = jnp.float32
BF16 = jnp.bfloat16

N_META = 16
RWKV_HEAD = 64
SB_HEAD = 128
LN_EPS = 1e-5
GN_EPS = 64e-5
L2_EPS = 1e-12
DEPTH = 2
DEEPNORM_ALPHA = (2 * DEPTH) ** 0.25

LANES = 128
MXU_COLS = 256
CHUNK = 64
SEQ_ALIGN = 128
VMEM_LIMIT = 56 * 1024 * 1024


def _cparams(sem):
    return pltpu.CompilerParams(dimension_semantics=sem, vmem_limit_bytes=VMEM_LIMIT)


def _pick(n, prefs):
    for p in prefs:
        if n % p == 0:
            return p
    return n


def _mm_kernel(a_ref, w_ref, o_ref, *, nk, act):
    if nk == 1:
        acc = jnp.dot(a_ref[...], w_ref[...].astype(BF16), preferred_element_type=F32)
        if act == "relu2":
            acc = jnp.square(jnp.maximum(acc, 0.0))
        o_ref[...] = acc.astype(o_ref.dtype)
        return

    tn = o_ref.shape[1]
    cw = _pick(tn, (MXU_COLS, LANES))

    def chunks(first):
        a = a_ref[...]
        for c in range(tn // cw):
            cols = slice(c * cw, (c + 1) * cw)
            part = jnp.dot(a, w_ref[:, cols].astype(BF16), preferred_element_type=F32)
            o_ref[:, cols] = part if first else o_ref[:, cols] + part

    k = pl.program_id(2)
    pl.when(k == 0)(functools.partial(chunks, True))
    pl.when(k > 0)(functools.partial(chunks, False))


def _matmul(a, w, *, a_prefix=(), w_prefix=(), act="none", out_dtype=F32,
            tm=None, tn=None, tk=None):
    M, K = a.shape[-2:]
    N = w.shape[-1]
    tm = _pick(M, (tm or 1088, 1024, 512, 256, 128, 64, 32, 16))
    tn = _pick(N, (tn or 512, 512, 256, 128))
    tk = _pick(K, (tk or 4096, 2048, 1024, 512, 256, 128))
    nk = K // tk
    na, nw = len(a_prefix), len(w_prefix)
    a_spec = pl.BlockSpec((None,) * na + (tm, tk), lambda i, j, k: a_prefix + (i, k))
    w_spec = pl.BlockSpec((None,) * nw + (tk, tn), lambda i, j, k: w_prefix + (k, j))
    assert nk == 1 or (act == "none" and out_dtype == F32), "K-tiled path accumulates in the f32 output"
    return pl.pallas_call(
        functools.partial(_mm_kernel, nk=nk, act=act),
        out_shape=jax.ShapeDtypeStruct((M, N), out_dtype),
        grid=(M // tm, N // tn, nk),
        in_specs=[a_spec, w_spec],
        out_specs=pl.BlockSpec((tm, tn), lambda i, j, k: (i, j)),
        compiler_params=_cparams(("parallel", "parallel", "arbitrary")),
        name="matmul_" + act,
    )(a, w)


def _lora_kernel(a_ref, w1_ref, w2_ref, o_ref, *, act):
    t = jnp.dot(a_ref[...], w1_ref[...], preferred_element_type=F32)
    if act == "tanh":
        t = jnp.tanh(t)
    elif act == "sigmoid":
        t = 1.0 / (1.0 + jnp.exp(-t))
    o_ref[...] = jnp.dot(t.astype(BF16), w2_ref[...], preferred_element_type=F32)


def _lora(a, w1, w2, *, a_prefix, act):
    M, K = a.shape[-2:]
    R = w1.shape[-1]
    N = w2.shape[-1]
    tm = _pick(M, (544, 512, 256, 128, 64, 32, 16))
    na = len(a_prefix)
    return pl.pallas_call(
        functools.partial(_lora_kernel, act=act),
        out_shape=jax.ShapeDtypeStruct((M, N), F32),
        grid=(M // tm,),
        in_specs=[
            pl.BlockSpec((None,) * na + (tm, K), lambda i: a_prefix + (i, 0)),
            pl.BlockSpec((K, R), lambda i: (0, 0)),
            pl.BlockSpec((R, N), lambda i: (0, 0)),
        ],
        out_specs=pl.BlockSpec((tm, N), lambda i: (i, 0)),
        compiler_params=_cparams(("parallel",)),
        name="lora_" + act,
    )(a, w1, w2)


def _mix_kernel(x_ref, xp_ref, mu_ref, o_ref):
    x = x_ref[0]
    tt = x.shape[0]
    first = pl.program_id(1) == 0
    prev_row = jnp.where(first, 0.0, xp_ref[0, 7:8, :])
    rolled = pltpu.roll(x, 1, 0)
    row = lax.broadcasted_iota(jnp.int32, (tt, 1), 0)
    xprev = jnp.where(row == 0, prev_row, rolled)
    xx = xprev - x
    for n in range(o_ref.shape[0]):
        o_ref[n, 0] = (x + xx * mu_ref[n:n + 1, :]).astype(o_ref.dtype)


def _rwkv_mix(h, mu):
    B, T, D = h.shape
    n_mu = mu.shape[0]
    tt = _pick(T, (272, 256, 128, 64, 32, 16, 8))
    sub = tt // 8
    return pl.pallas_call(
        _mix_kernel,
        out_shape=jax.ShapeDtypeStruct((n_mu, B, T, D), BF16),
        grid=(B, T // tt),
        in_specs=[
            pl.BlockSpec((1, tt, D), lambda b, i: (b, i, 0)),
            pl.BlockSpec((1, 8, D), lambda b, i: (b, jnp.maximum(i * sub - 1, 0), 0)),
            pl.BlockSpec((n_mu, D), lambda b, i: (0, 0)),
        ],
        out_specs=pl.BlockSpec((n_mu, 1, tt, D), lambda b, i: (0, b, i, 0)),
        compiler_params=_cparams(("parallel", "parallel")),
        name="rwkv_mix",
    )(h, h, mu)


def _ln_kernel(h_ref, y_ref, g_ref, b_ref, o_ref, ob_ref):
    z = DEEPNORM_ALPHA * h_ref[...] + y_ref[...]
    mu = jnp.mean(z, axis=-1, keepdims=True)
    zc = z - mu
    var = jnp.mean(jnp.square(zc), axis=-1, keepdims=True)
    out = zc * lax.rsqrt(var + LN_EPS) * g_ref[...] + b_ref[...]
    o_ref[...] = out
    ob_ref[...] = out.astype(BF16)


def _add_ln(h, y, g, b):
    M, D = h.shape
    tm = _pick(M, (272, 256, 128, 64, 32, 16, 8))
    row = pl.BlockSpec((tm, D), lambda i: (i, 0))
    vec = pl.BlockSpec((1, D), lambda i: (0, 0))
    return pl.pallas_call(
        _ln_kernel,
        out_shape=(jax.ShapeDtypeStruct((M, D), F32), jax.ShapeDtypeStruct((M, D), BF16)),
        grid=(M // tm,),
        in_specs=[row, row, vec, vec],
        out_specs=(row, row),
        compiler_params=_cparams(("parallel",)),
        name="add_layernorm",
    )(h, y, g.reshape(1, D), b.reshape(1, D))


def _split3(x):
    hi = x.astype(BF16)
    r1 = x - hi.astype(F32)
    mid = r1.astype(BF16)
    lo = (r1 - mid.astype(F32)).astype(BF16)
    return hi, mid, lo


def _split2(x):
    hi = x.astype(BF16)
    lo = (x - hi.astype(F32)).astype(BF16)
    return hi, lo


def _dot(a, b):
    return jnp.dot(a, b, preferred_element_type=F32)


def _dot_nt(a, b):
    return lax.dot_general(a, b, (((1,), (1,)), ((), ())), preferred_element_type=F32)


def _dot_tn(a, b):
    return lax.dot_general(a, b, (((0,), (0,)), ((), ())), preferred_element_type=F32)


def _scan_kernel(r_ref, k_ref, v_ref, wl_ref, al_ref, g_ref,
                 w0_ref, a0_ref, kk_ref, ka_ref, rk_ref, gg_ref, gb_ref,
                 o_ref, s_ref):
    L = CHUNK
    hb = r_ref.shape[-1]
    n_pairs = hb // LANES
    P = 2 * L

    @pl.when(pl.program_id(2) == 0)
    def _():
        s_ref[...] = jnp.zeros_like(s_ref)

    r = r_ref[0]
    k = k_ref[0]
    v = v_ref[0]

    wpre = w0_ref[...] + wl_ref[0]
    nsp = jnp.maximum(-wpre, 0.0) + jnp.log1p(jnp.exp(-jnp.abs(wpre)))
    lw = -jnp.exp(-nsp - 0.5)
    a = 1.0 / (1.0 + jnp.exp(-(a0_ref[...] + al_ref[0])))

    li = lax.broadcasted_iota(jnp.int32, (2 * LANES, LANES), 0)
    lj = lax.broadcasted_iota(jnp.int32, (2 * LANES, LANES), 1)
    seg_ones = jnp.where(((li % LANES) // RWKV_HEAD) == (lj // RWKV_HEAD), 1.0, 0.0).astype(BF16)
    ti = lax.broadcasted_iota(jnp.int32, (L, L), 0)
    tj = lax.broadcasted_iota(jnp.int32, (L, L), 1)
    tril_incl = jnp.where(tj <= ti, 1.0, 0.0).astype(BF16)

    def to_rows(x):
        return jnp.concatenate([x[:, p * LANES:(p + 1) * LANES] for p in range(n_pairs)], axis=0)

    def to_lanes(x):
        return jnp.concatenate([x[p * L:(p + 1) * L] for p in range(n_pairs)], axis=1)

    def segsum(x):
        hi, lo = _split2(to_rows(x))
        return to_lanes(_dot(jnp.concatenate([hi, lo], axis=1), seg_ones))

    parts = _split3(lw)
    cs = _dot(tril_incl, jnp.concatenate(parts, axis=1))
    G = cs[:, :hb] + cs[:, hb:2 * hb] + cs[:, 2 * hb:]
    GL = G[L - 1:L, :]
    enG = jnp.exp(-G)
    eGL = jnp.exp(GL)
    eTail = jnp.exp(GL - G)

    kkr = k * kk_ref[...]
    kk = kkr / jnp.maximum(jnp.sqrt(segsum(kkr * kkr)), L2_EPS)
    k2 = k * (1.0 + (a - 1.0) * ka_ref[...])
    b = kk * a
    full = dict(R=r * jnp.exp(G), Q=kk * jnp.exp(G - lw), Kt=k2 * enG, Bt=b * enG,
                Kb=k2 * eTail, Bb=b * eTail, V=v)

    pi = lax.broadcasted_iota(jnp.int32, (P, P), 0)
    pj = lax.broadcasted_iota(jnp.int32, (P, P), 1)
    strict = (pj % L) < (pi % L)
    incl = (pj % L) <= (pi % L)
    eye = jnp.where(pi == pj, 1.0, 0.0)
    lane = lax.broadcasted_iota(jnp.int32, (L, LANES), 1)
    head0 = lane < RWKV_HEAD

    def stack(x):
        return jnp.concatenate([jnp.where(head0, x, 0.0), jnp.where(head0, 0.0, x)], axis=0).astype(BF16)

    pairs = range(n_pairs)
    st = {name: [stack(x[:, p * LANES:(p + 1) * LANES]) for p in pairs] for name, x in full.items()}
    S = [s_ref[p] for p in pairs]

    AA = [_dot_nt(jnp.concatenate([st["Q"][p], st["R"][p]], axis=0),
                  jnp.concatenate([st["Bt"][p], st["Kt"][p]], axis=0)) for p in pairs]
    A_qk = [jnp.where(strict, AA[p][:P, P:], 0.0).astype(BF16) for p in pairs]
    A_r = [jnp.concatenate([jnp.where(incl, AA[p][P:, P:], 0.0).astype(BF16),
                            jnp.where(incl, -AA[p][P:, :P], 0.0).astype(BF16)], axis=1) for p in pairs]
    AV = [_dot(A_qk[p], st["V"][p]) for p in pairs]

    Npow = [jnp.where(strict, -AA[p][:P, :P], 0.0) for p in pairs]
    T = [eye + Npow[p] for p in pairs]
    for _ in range((L - 1).bit_length() - 1):
        nb = [Npow[p].astype(BF16) for p in pairs]
        Npow = [_dot(nb[p], nb[p]) for p in pairs]
        T = [T[p] + _dot(T[p].astype(BF16), Npow[p].astype(BF16)) for p in pairs]

    WU = [_dot(T[p].astype(BF16), jnp.concatenate([st["Q"][p], AV[p].astype(BF16)], axis=1)) for p in pairs]
    WR = [_dot_nt(jnp.concatenate([WU[p][:, :LANES].astype(BF16), st["R"][p]], axis=0), S[p].astype(BF16))
          for p in pairs]
    Ub = [(WR[p][:P] + WU[p][:, LANES:]).astype(BF16) for p in pairs]
    Ys = [WR[p][P:] + _dot(A_r[p], jnp.concatenate([st["V"][p], Ub[p]], axis=0)) for p in pairs]
    for p in pairs:
        s_ref[p] = S[p] * eGL[:, p * LANES:(p + 1) * LANES] + _dot_tn(
            jnp.concatenate([st["V"][p], -Ub[p]], axis=0), jnp.concatenate([st["Kb"][p], st["Bb"][p]], axis=0))
    y = jnp.concatenate([Ys[p][:L] + Ys[p][L:] for p in pairs], axis=1)

    mu = segsum(y) * (1.0 / RWKV_HEAD)
    yc = y - mu
    var = segsum(yc * yc) * (1.0 / RWKV_HEAD)
    y_n = yc * lax.rsqrt(var + GN_EPS) * gg_ref[...] + gb_ref[...]
    bonus = segsum(r * k2 * rk_ref[...]) * v
    o_ref[0] = ((y_n + bonus) * g_ref[0]).astype(o_ref.dtype)


def _rwkv_scan(r, k, v, wl, al, g, w0, a0, k_k, k_a, r_k, gn_g, gn_b):
    B, T, D = r.shape
    hb = _pick(D, (2048, 1024, 512, 256, 128))
    seq =pl.BlockSpec((1, CHUNK, hb), lambda b, h, c: (b, c, h))
    vec = pl.BlockSpec((1, hb), lambda b, h, c: (0, h))
    vecs = [x.reshape(1, D) for x in (w0, a0, k_k, k_a, r_k, gn_g, gn_b)]
    return pl.pallas_call(
        _scan_kernel,
        out_shape=jax.ShapeDtypeStruct((B, T, D), BF16),
        grid=(B, D // hb, T // CHUNK),
        in_specs=[seq] * 6 + [vec] * 7,
        out_specs=seq,
        scratch_shapes=[pltpu.VMEM((hb // LANES, LANES, LANES), F32)],
        compiler_params=_cparams(("parallel", "parallel", "arbitrary")),
        name="rwkv7_scan",
    )(r, k, v, wl, al, g, *vecs)


def _sb_kernel(q_ref, k_ref, v_ref, cm_ref, o_ref):
    tq = q_ref.shape[1]
    tk = tq
    n_h = q_ref.shape[2] // SB_HEAD
    heads = range(n_h)
    qi = pl.program_id(2)
    scale = SB_HEAD ** -0.5
    q = [q_ref[0, :, g * SB_HEAD:(g + 1) * SB_HEAD] for g in heads]
    cm = cm_ref[...]

    def step(kb, nb, diag, carry):
        w = nb * tk
        rows = pl.ds(pl.multiple_of(kb * tk, tk), w)
        if diag:
            ri = lax.broadcasted_iota(jnp.int32, (tq, w), 0)
            ci = lax.broadcasted_iota(jnp.int32, (tq, w), 1)
            causal = ci < ri + (w - tk)
        zn, lk = [], []
        for g in heads:
            ks = k_ref[0, rows, g * SB_HEAD:(g + 1) * SB_HEAD]
            zng = _dot_nt(q[g], ks) * (-scale)
            neg_abs = lax.bitcast_convert_type(
                lax.bitcast_convert_type(zng, jnp.int32) | jnp.int32(-2 ** 31), F32)
            lkg = jnp.minimum(zng, 0.0) - jnp.log(jnp.maximum(1.0 + jnp.exp(neg_abs), 1.0))
            if diag:
                lkg = jnp.where(causal, lkg, 0.0)
            zn.append(zng)
            lk.extend(lkg[:, j * tk:(j + 1) * tk] for j in range(nb))
        hi, lo = _split2(jnp.concatenate(lk, axis=0))
        sums = _dot(jnp.concatenate([hi, lo], axis=1), cm)
        out = []
        for g in heads:
            c, acc = carry[g]
            sg = [sums[(g * nb + j) * tq:(g * nb + j + 1) * tq] for j in range(nb)]
            cs = [c]
            for j in range(nb - 1, 0, -1):
                cs.insert(0, cs[0] + sg[j][:, tk:])
            logit = jnp.concatenate([sg[j][:, :tk] + cs[j] for j in range(nb)], axis=1) - zn[g]
            pexp = jnp.exp(logit)
            if diag:
                pexp = jnp.where(causal, pexp, 0.0)
            vs = v_ref[0, rows, g * SB_HEAD:(g + 1) * SB_HEAD]
            out.append((cs[0] + sg[0][:, tk:], acc + _dot(pexp.astype(BF16), vs)))
        return tuple(out)

    init = tuple((jnp.zeros((tq, tk), F32), jnp.zeros((tq, SB_HEAD), F32)) for _ in heads)
    n_pairs = qi // 2
    carry = lax.cond(qi % 2 == 1,
                     lambda: step(qi - 1, 2, True, init),
                     lambda: step(qi, 1, True, init))
    carry = lax.fori_loop(0, n_pairs, lambda n, cr: step(2 * (n_pairs - 1 - n), 2, False, cr), carry)
    o_ref[0] = jnp.concatenate([carry[g][1] for g in heads], axis=1).astype(o_ref.dtype)


def _sb_attention(qkv, n_heads):
    B, T, _ = qkv.shape
    tq = SB_HEAD
    gh = _pick(n_heads, (8, 4, 2, 1))
    ng = n_heads // gh
    ri = jnp.arange(2 * tq)[:, None] % tq
    ci = jnp.arange(2 * tq)[None, :]
    cm = jnp.where(ci < tq, ri >= ci, True).astype(BF16)
    return pl.pallas_call(
        _sb_kernel,
        out_shape=jax.ShapeDtypeStruct((B, T, n_heads * SB_HEAD), BF16),
        grid=(B, ng, T // tq),
        in_specs=[
            pl.BlockSpec((1, tq, gh * SB_HEAD), lambda b, h, i: (b, i, h)),
            pl.BlockSpec((1, T, gh * SB_HEAD), lambda b, h, i: (b, 0, ng + h)),
            pl.BlockSpec((1, T, gh * SB_HEAD), lambda b, h, i: (b, 0, 2 * ng + h)),
            pl.BlockSpec((2 * tq, 2 * tq), lambda b, h, i: (0, 0)),
        ],
        out_specs=pl.BlockSpec((1, tq, gh * SB_HEAD), lambda b, h, i: (b, i, h)),
        compiler_params=_cparams(("parallel", "parallel", "arbitrary")),
        name="stick_breaking_attention",
    )(qkv, qkv, qkv, cm)


def kernel(x, meta_tokens, ln_mix_g, ln_mix_b, ln_ffn_g, ln_ffn_b, w_up, w_down, rwkv_mu, rwkv_w_rkv, rwkv_w0, rwkv_w1, rwkv_w2, rwkv_a0, rwkv_a1, rwkv_a2, rwkv_g1, rwkv_g2, rwkv_k_k, rwkv_k_a, rwkv_r_k, rwkv_gn_g, rwkv_gn_b, rwkv_w_o, sb_w_qkv, sb_w_o):
    B, seq_len, D = x.shape
    depth = w_up.shape[0]
    T = N_META + seq_len
    Tp = -(-T // SEQ_ALIGN) * SEQ_ALIGN
    M = B * Tp
    meta = jnp.broadcast_to(meta_tokens[None].astype(x.dtype), (B, N_META, D))
    h = jnp.concatenate([meta, x, jnp.zeros((B, Tp - T, D), x.dtype)], axis=1).reshape(M, D)
    hb = None
    for i in range(depth):
        j = i // 2
        if i % 2 == 0:
            mixed = _rwkv_mix(h.reshape(B, Tp, D), rwkv_mu[j]).reshape(-1, M, D)
            r, k, v = (_matmul(mixed, rwkv_w_rkv, a_prefix=(n,), w_prefix=(j, n)) for n in range(3))
            wl = _lora(mixed, rwkv_w1[j].astype(BF16), rwkv_w2[j].astype(BF16), a_prefix=(3,), act="tanh")
            al = _lora(mixed, rwkv_a1[j].astype(BF16), rwkv_a2[j].astype(BF16), a_prefix=(4,), act="none")
            g = _lora(mixed, rwkv_g1[j].astype(BF16), rwkv_g2[j].astype(BF16), a_prefix=(5,), act="sigmoid")
            r, k, v, wl, al, g = (z.reshape(B, Tp, D) for z in (r, k, v, wl, al, g))
            y = _rwkv_scan(r, k, v, wl, al, g, rwkv_w0[j], rwkv_a0[j], rwkv_k_k[j], rwkv_k_a[j],
                           rwkv_r_k[j], rwkv_gn_g[j], rwkv_gn_b[j])
            mix = _matmul(y.reshape(M, D), rwkv_w_o, w_prefix=(j,))
        else:
            qkv = _matmul(hb, sb_w_qkv, w_prefix=(j,), out_dtype=BF16)
            o = _sb_attention(qkv.reshape(B, Tp, 3 * D), D // SB_HEAD)
            mix = _matmul(o.reshape(M, D), sb_w_o, w_prefix=(j,))
        h, hb = _add_ln(h, mix, ln_mix_g[i], ln_mix_b[i])
        up = _matmul(hb, w_up, w_prefix=(i,), act="relu2", out_dtype=BF16)
        down = _matmul(up, w_down, w_prefix=(i,), tk=2048, tn=1024)
        h, hb = _add_ln(h, down, ln_ffn_g[i], ln_ffn_b[i])
    return h.reshape(B, Tp, D)[:, N_META:T]
```

```python
import functools

import jax
import jax.numpy as jnp
from jax import lax
from jax.experimental import pallas as pl
from jax.experimental.pallas import tpu as pltpu

F32 = jnp.float32
BF16 = jnp.bfloat16

N_META = 16
RWKV_HEAD = 64
SB_HEAD = 128
LN_EPS = 1e-5
GN_EPS = 64e-5
L2_EPS = 1e-12
DEPTH = 2
DEEPNORM_ALPHA = (2 * DEPTH) ** 0.25

LANES = 128
MXU_COLS = 256
CHUNK = 64
VMEM_LIMIT = 56 * 1024 * 1024


def _cparams(sem):
    return pltpu.CompilerParams(dimension_semantics=sem, vmem_limit_bytes=VMEM_LIMIT)


def _pick(n, prefs):
    for p in prefs:
        if n % p == 0:
            return p
    return n


def _mm_kernel(a_ref, w_ref, o_ref, *, nk, act):
    if nk == 1:
        acc = jnp.dot(a_ref[...], w_ref[...].astype(BF16), preferred_element_type=F32)
        if act == "relu2":
            acc = jnp.square(jnp.maximum(acc, 0.0))
        o_ref[...] = acc.astype(o_ref.dtype)
        return

    tn = o_ref.shape[1]
    cw = _pick(tn, (MXU_COLS, LANES))

    def chunks(first):
        a = a_ref[...]
        for c in range(tn // cw):
            cols = slice(c * cw, (c + 1) * cw)
            part = jnp.dot(a, w_ref[:, cols].astype(BF16), preferred_element_type=F32)
            o_ref[:, cols] = part if first else o_ref[:, cols] + part

    k = pl.program_id(2)
    pl.when(k == 0)(functools.partial(chunks, True))
    pl.when(k > 0)(functools.partial(chunks, False))


def _matmul(a, w, *, a_prefix=(), w_prefix=(), act="none", out_dtype=F32,
            tm=None, tn=None, tk=None):
    M, K = a.shape[-2:]
    N = w.shape[-1]
    tm = _pick(M, (tm or 1376, 1088, 1024, 512, 256, 128, 64, 32, 16))
    tn = _pick(N, (tn or 512, 512, 256, 128))
    tk = _pick(K, (tk or 4096, 2048, 1024, 512, 256, 128))
    nk = K // tk
    na, nw = len(a_prefix), len(w_prefix)
    a_spec = pl.BlockSpec((None,) * na + (tm, tk), lambda i, j, k: a_prefix + (i, k))
    w_spec = pl.BlockSpec((None,) * nw + (tk, tn), lambda i, j, k: w_prefix + (k, j))
    assert nk == 1 or (act == "none" and out_dtype == F32), "K-tiled path accumulates in the f32 output"
    return pl.pallas_call(
        functools.partial(_mm_kernel, nk=nk, act=act),
        out_shape=jax.ShapeDtypeStruct((M, N), out_dtype),
        grid=(M // tm, N // tn, nk),
        in_specs=[a_spec, w_spec],
        out_specs=pl.BlockSpec((tm, tn), lambda i, j, k: (i, j)),
        compiler_params=_cparams(("parallel", "parallel", "arbitrary")),
        name="matmul_" + act,
    )(a, w)


def _lora_kernel(a_ref, w1_ref, w2_ref, o_ref, *, act):
    t = jnp.dot(a_ref[...], w1_ref[...], preferred_element_type=F32)
    if act == "tanh":
        t = jnp.tanh(t)
    elif act == "sigmoid":
        t = 1.0 / (1.0 + jnp.exp(-t))
    o_ref[...] = jnp.dot(t.astype(BF16), w2_ref[...], preferred_element_type=F32)


def _lora(a, w1, w2, *, a_prefix, act):
    M, K = a.shape[-2:]
    R = w1.shape[-1]
    N = w2.shape[-1]
    tm = _pick(M, (688, 544, 512, 256, 128, 64, 32, 16))
    na = len(a_prefix)
    return pl.pallas_call(
        functools.partial(_lora_kernel, act=act),
        out_shape=jax.ShapeDtypeStruct((M, N), F32),
        grid=(M // tm,),
        in_specs=[
            pl.BlockSpec((None,) * na + (tm, K), lambda i: a_prefix + (i, 0)),
            pl.BlockSpec((K, R), lambda i: (0, 0)),
            pl.BlockSpec((R, N), lambda i: (0, 0)),
        ],
        out_specs=pl.BlockSpec((tm, N), lambda i: (i, 0)),
        compiler_params=_cparams(("parallel",)),
        name="lora_" + act,
    )(a, w1, w2)


def _mix_kernel(x_ref, xp_ref, mu_ref, o_ref):
    x = x_ref[0]
    tt = x.shape[0]
    first = pl.program_id(1) == 0
    prev_row = jnp.where(first, 0.0, xp_ref[0, 7:8, :])
    rolled = pltpu.roll(x, 1, 0)
    row = lax.broadcasted_iota(jnp.int32, (tt, 1), 0)
    xprev = jnp.where(row == 0, prev_row, rolled)
    xx = xprev - x
    for n in range(o_ref.shape[0]):
        o_ref[n, 0] = (x + xx * mu_ref[n:n + 1, :]).astype(o_ref.dtype)


def _rwkv_mix(h, mu):
    B, T, D = h.shape
    n_mu = mu.shape[0]
    tt = _pick(T, (688, 256, 128, 64, 32, 16))
    dc = _pick(D, (1024, 512, 256, 128))
    sub = tt // 8
    return pl.pallas_call(
        _mix_kernel,
        out_shape=jax.ShapeDtypeStruct((n_mu, B, T, D), BF16),
        grid=(B, T // tt, D // dc),
        in_specs=[
            pl.BlockSpec((1, tt, dc), lambda b, i, d: (b, i, d)),
            pl.BlockSpec((1, 8, dc), lambda b, i, d: (b, jnp.maximum(i * sub - 1, 0), d)),
            pl.BlockSpec((n_mu, dc), lambda b, i, d: (0, d)),
        ],
        out_specs=pl.BlockSpec((n_mu, 1, tt, dc), lambda b, i, d: (0, b, i, d)),
        compiler_params=_cparams(("parallel", "parallel", "parallel")),
        name="rwkv_mix",
    )(h, h, mu)


def _ln_kernel(h_ref, y_ref, g_ref, b_ref, o_ref, ob_ref):
    z = DEEPNORM_ALPHA * h_ref[...] + y_ref[...]
    mu = jnp.mean(z, axis=-1, keepdims=True)
    zc = z - mu
    var = jnp.mean(jnp.square(zc), axis=-1, keepdims=True)
    out = zc * lax.rsqrt(var + LN_EPS) * g_ref[...] + b_ref[...]
    o_ref[...] = out
    ob_ref[...] = out.astype(BF16)


def _add_ln(h, y, g, b):
    M, D = h.shape
    tm = _pick(M, (272, 256, 192, 128, 64, 32, 16, 8))
    row = pl.BlockSpec((tm, D), lambda i: (i, 0))
    vec = pl.BlockSpec((1, D), lambda i: (0, 0))
    return pl.pallas_call(
        _ln_kernel,
        out_shape=(jax.ShapeDtypeStruct((M, D), F32), jax.ShapeDtypeStruct((M, D), BF16)),
        grid=(M // tm,),
        in_specs=[row, row, vec, vec],
        out_specs=(row, row),
        compiler_params=_cparams(("parallel",)),
        name="add_layernorm",
    )(h, y, g.reshape(1, D), b.reshape(1, D))


def _split3(x):
    hi = x.astype(BF16)
    r1 = x - hi.astype(F32)
    mid = r1.astype(BF16)
    lo = (r1 - mid.astype(F32)).astype(BF16)
    return hi, mid, lo


def _split2(x):
    hi = x.astype(BF16)
    lo = (x - hi.astype(F32)).astype(BF16)
    return hi, lo


def _dot(a, b):
    return jnp.dot(a, b, preferred_element_type=F32)


def _dot_nt(a, b):
    return lax.dot_general(a, b, (((1,), (1,)), ((), ())), preferred_element_type=F32)


def _dot_tn(a, b):
    return lax.dot_general(a, b, (((0,), (0,)), ((), ())), preferred_element_type=F32)


def _scan_kernel(r_ref, k_ref, v_ref, wl_ref, al_ref, g_ref,
                 w0_ref, a0_ref, kk_ref, ka_ref, rk_ref, gg_ref, gb_ref,
                 o_ref, s_ref, *, seq_len):
    L = CHUNK
    hb = r_ref.shape[-1]
    n_pairs = hb // LANES
    P = 2 * L

    @pl.when(pl.program_id(2) == 0)
    def _():
        s_ref[...] = jnp.zeros_like(s_ref)

    if seq_len % L:
        row = pl.program_id(2) * L + lax.broadcasted_iota(jnp.int32, (L, 1), 0)
        load = lambda ref: jnp.where(row < seq_len, ref[0], 0.0)
    else:
        load = lambda ref: ref[0]
    r = load(r_ref)
    k = load(k_ref)
    v = load(v_ref)

    wpre = w0_ref[...] + load(wl_ref)
    nsp = jnp.maximum(-wpre, 0.0) + jnp.log1p(jnp.exp(-jnp.abs(wpre)))
    lw = -jnp.exp(-nsp - 0.5)
    a = 1.0 / (1.0 + jnp.exp(-(a0_ref[...] + load(al_ref))))

    li = lax.broadcasted_iota(jnp.int32, (2 * LANES, LANES), 0)
    lj = lax.broadcasted_iota(jnp.int32, (2 * LANES, LANES), 1)
    seg_ones = jnp.where(((li % LANES) // RWKV_HEAD) == (lj // RWKV_HEAD), 1.0, 0.0).astype(BF16)
    ti = lax.broadcasted_iota(jnp.int32, (L, L), 0)
    tj = lax.broadcasted_iota(jnp.int32, (L, L), 1)
    tril_incl = jnp.where(tj <= ti, 1.0, 0.0).astype(BF16)

    def to_rows(x):
        return jnp.concatenate([x[:, p * LANES:(p + 1) * LANES] for p in range(n_pairs)], axis=0)

    def to_lanes(x):
        return jnp.concatenate([x[p * L:(p + 1) * L] for p in range(n_pairs)], axis=1)

    def segsum(x):
        hi, lo = _split2(to_rows(x))
        return to_lanes(_dot(jnp.concatenate([hi, lo], axis=1), seg_ones))

    parts = _split3(lw)
    cs = _dot(tril_incl, jnp.concatenate(parts, axis=1))
    G = cs[:, :hb] + cs[:, hb:2 * hb] + cs[:, 2 * hb:]
    GL = G[L - 1:L, :]
    enG = jnp.exp(-G)
    eGL = jnp.exp(GL)
    eTail = jnp.exp(GL - G)

    kkr = k * kk_ref[...]
    kk = kkr / jnp.maximum(jnp.sqrt(segsum(kkr * kkr)), L2_EPS)
    k2 = k * (1.0 + (a - 1.0) * ka_ref[...])
    b = kk * a
    full = dict(R=r * jnp.exp(G), Q=kk * jnp.exp(G - lw), Kt=k2 * enG, Bt=b * enG,
                Kb=k2 * eTail, Bb=b * eTail, V=v)

    pi = lax.broadcasted_iota(jnp.int32, (P, P), 0)
    pj = lax.broadcasted_iota(jnp.int32, (P, P), 1)
    strict = (pj % L) < (pi % L)
    incl = (pj % L) <= (pi % L)
    eye = jnp.where(pi == pj, 1.0, 0.0)
    lane = lax.broadcasted_iota(jnp.int32, (L, LANES), 1)
    head0 = lane < RWKV_HEAD

    def stack(x):
        return jnp.concatenate([jnp.where(head0, x, 0.0), jnp.where(head0, 0.0, x)], axis=0).astype(BF16)

    pairs = range(n_pairs)
    st = {name: [stack(x[:, p * LANES:(p + 1) * LANES]) for p in pairs] for name, x in full.items()}
    S = [s_ref[p] for p in pairs]

    AA = [_dot_nt(jnp.concatenate([st["Q"][p], st["R"][p]], axis=0),
                  jnp.concatenate([st["Bt"][p], st["Kt"][p]], axis=0)) for p in pairs]
    A_qk = [jnp.where(strict, AA[p][:P, P:], 0.0).astype(BF16) for p in pairs]
    A_r = [jnp.concatenate([jnp.where(incl, AA[p][P:, P:], 0.0).astype(BF16),
                            jnp.where(incl, -AA[p][P:, :P], 0.0).astype(BF16)], axis=1) for p in pairs]
    AV = [_dot(A_qk[p], st["V"][p]) for p in pairs]

    Npow = [jnp.where(strict, -AA[p][:P, :P], 0.0) for p in pairs]
    T = [eye + Npow[p] for p in pairs]
    for _ in range((L - 1).bit_length() - 1):
        nb = [Npow[p].astype(BF16) for p in pairs]
        Npow = [_dot(nb[p], nb[p]) for p in pairs]
        T = [T[p] + _dot(T[p].astype(BF16), Npow[p].astype(BF16)) for p in pairs]

    WU = [_dot(T[p].astype(BF16), jnp.concatenate([st["Q"][p], AV[p].astype(BF16)], axis=1)) for p in pairs]
    WR = [_dot_nt(jnp.concatenate([WU[p][:, :LANES].astype(BF16), st["R"][p]], axis=0), S[p].astype(BF16))
          for p in pairs]
    Ub = [(WR[p][:P] + WU[p][:, LANES:]).astype(BF16) for p in pairs]
    Ys = [WR[p][P:] + _dot(A_r[p], jnp.concatenate([st["V"][p], Ub[p]], axis=0)) for p in pairs]
    for p in pairs:
        s_ref[p] = S[p] * eGL[:, p * LANES:(p + 1) * LANES] + _dot_tn(
            jnp.concatenate([st["V"][p], -Ub[p]], axis=0), jnp.concatenate([st["Kb"][p], st["Bb"][p]], axis=0))
    y = jnp.concatenate([Ys[p][:L] + Ys[p][L:] for p in pairs], axis=1)

    mu = segsum(y) * (1.0 / RWKV_HEAD)
    yc = y - mu
    var = segsum(yc * yc) * (1.0 / RWKV_HEAD)
    y_n = yc * lax.rsqrt(var + GN_EPS) * gg_ref[...] + gb_ref[...]
    bonus = segsum(r * k2 * rk_ref[...]) * v
    o_ref[0] = ((y_n + bonus) * load(g_ref)).astype(o_ref.dtype)


def _rwkv_scan(r, k, v, wl, al, g, w0, a0, k_k, k_a, r_k, gn_g, gn_b):
    B, T, D = r.shape
    hb = _pick(D, (2048, 1024, 512, 256, 128))
    seq =pl.BlockSpec((1, CHUNK, hb), lambda b, h, c: (b, c, h))
    vec = pl.BlockSpec((1, hb), lambda b, h, c: (0, h))
    vecs = [x.reshape(1, D) for x in (w0, a0, k_k, k_a, r_k, gn_g, gn_b)]
    return pl.pallas_call(
        functools.partial(_scan_kernel, seq_len=T),
        out_shape=jax.ShapeDtypeStruct((B, T, D), BF16),
        grid=(B, D // hb, pl.cdiv(T, CHUNK)),
        in_specs=[seq] * 6 + [vec] * 7,
        out_specs=seq,
        scratch_shapes=[pltpu.VMEM((hb // LANES, LANES, LANES), F32)],
        compiler_params=_cparams(("parallel", "parallel", "arbitrary")),
        name="rwkv7_scan",
    )(r, k, v, wl, al, g, *vecs)


def _sb_kernel(q_ref, k_ref, v_ref, cm_ref, o_ref, *, seq_len):
    tq = q_ref.shape[1]
    tk = tq
    n_h = q_ref.shape[2] // SB_HEAD
    heads = range(n_h)
    qi = pl.program_id(2)
    scale = SB_HEAD ** -0.5
    q = [q_ref[0, :, g * SB_HEAD:(g + 1) * SB_HEAD] for g in heads]
    cm = cm_ref[...]

    def step(kb, nb, diag, carry, tail=0):
        w = nb * tk
        if tail:
            def load(ref, cols):
                x = ref[0, kb * tk:kb * tk + tail, cols]
                return jnp.concatenate([x, jnp.zeros((tk - tail, SB_HEAD), x.dtype)], axis=0)
        else:
            rows = pl.ds(pl.multiple_of(kb * tk, tk), w)
            load = lambda ref, cols: ref[0, rows, cols]
        if diag:
            ri = lax.broadcasted_iota(jnp.int32, (tq, w), 0)
            ci = lax.broadcasted_iota(jnp.int32, (tq, w), 1)
            causal = ci < ri + (w - tk)
        zn, lk = [], []
        for g in heads:
            ks = load(k_ref, slice(g * SB_HEAD, (g + 1) * SB_HEAD))
            zng = _dot_nt(q[g], ks) * (-scale)
            neg_abs = lax.bitcast_convert_type(
                lax.bitcast_convert_type(zng, jnp.int32) | jnp.int32(-2 ** 31), F32)
            lkg = jnp.minimum(zng, 0.0) - jnp.log(jnp.maximum(1.0 + jnp.exp(neg_abs), 1.0))
            if diag:
                lkg = jnp.where(causal, lkg, 0.0)
            zn.append(zng)
            lk.extend(lkg[:, j * tk:(j + 1) * tk] for j in range(nb))
        hi, lo = _split2(jnp.concatenate(lk, axis=0))
        sums = _dot(jnp.concatenate([hi, lo], axis=1), cm)
        out = []
        for g in heads:
            c, acc = carry[g]
            sg = [sums[(g * nb + j) * tq:(g * nb + j + 1) * tq] for j in range(nb)]
            cs = [c]
            for j in range(nb - 1, 0, -1):
                cs.insert(0, cs[0] + sg[j][:, tk:])
            logit = jnp.concatenate([sg[j][:, :tk] + cs[j] for j in range(nb)], axis=1) - zn[g]
            pexp = jnp.exp(logit)
            if diag:
                pexp = jnp.where(causal, pexp, 0.0)
            vs = load(v_ref, slice(g * SB_HEAD, (g + 1) * SB_HEAD))
            out.append((cs[0] + sg[0][:, tk:], acc + _dot(pexp.astype(BF16), vs)))
        return tuple(out)

    init = tuple((jnp.zeros((tq, tk), F32), jnp.zeros((tq, SB_HEAD), F32)) for _ in heads)
    n_pairs = qi // 2

    def diag_step():
        return lax.cond(qi % 2 == 1,
                        lambda: step(qi - 1, 2, True, init),
                        lambda: step(qi, 1, True, init))

    last, tail = divmod(seq_len, tq)
    if tail:
        assert last % 2 == 0, "ragged last block must follow an even number of full blocks"
        carry = lax.cond(qi == last, lambda: step(last, 1, True, init, tail=tail), diag_step)
    else:
        carry = diag_step()
    carry = lax.fori_loop(0, n_pairs, lambda n, cr: step(2 * (n_pairs - 1 - n), 2, False, cr), carry)
    o_ref[0] = jnp.concatenate([carry[g][1] for g in heads], axis=1).astype(o_ref.dtype)


def _sb_attention(qkv, n_heads):
    B, T, _ = qkv.shape
    tq = SB_HEAD
    gh = _pick(n_heads, (8, 4, 2, 1))
    ng = n_heads // gh
    ri = jnp.arange(2 * tq)[:, None] % tq
    ci = jnp.arange(2 * tq)[None, :]
    cm = jnp.where(ci < tq, ri >= ci, True).astype(BF16)
    return pl.pallas_call(
        functools.partial(_sb_kernel, seq_len=T),
        out_shape=jax.ShapeDtypeStruct((B, T, n_heads * SB_HEAD), BF16),
        grid=(B, ng, pl.cdiv(T, tq)),
        in_specs=[
            pl.BlockSpec((1, tq, gh * SB_HEAD), lambda b, h, i: (b, i, h)),
            pl.BlockSpec((1, T, gh * SB_HEAD), lambda b, h, i: (b, 0, ng + h)),
            pl.BlockSpec((1, T, gh * SB_HEAD), lambda b, h, i: (b, 0, 2 * ng + h)),
            pl.BlockSpec((2 * tq, 2 * tq), lambda b, h, i: (0, 0)),
        ],
        out_specs=pl.BlockSpec((1, tq, gh * SB_HEAD), lambda b, h, i: (b, i, h)),
        compiler_params=_cparams(("parallel", "parallel", "arbitrary")),
        name="stick_breaking_attention",
    )(qkv, qkv, qkv, cm)


def kernel(x, meta_tokens, ln_mix_g, ln_mix_b, ln_ffn_g, ln_ffn_b, w_up, w_down, rwkv_mu, rwkv_w_rkv, rwkv_w0, rwkv_w1, rwkv_w2, rwkv_a0, rwkv_a1, rwkv_a2, rwkv_g1, rwkv_g2, rwkv_k_k, rwkv_k_a, rwkv_r_k, rwkv_gn_g, rwkv_gn_b, rwkv_w_o, sb_w_qkv, sb_w_o):
    B, seq_len, D = x.shape
    depth = w_up.shape[0]
    T = N_META + seq_len
    M = B * T
    meta = jnp.broadcast_to(meta_tokens[None].astype(x.dtype), (B, N_META, D))
    h = jnp.concatenate([meta, x], axis=1).reshape(M, D)
    hb = None
    for i in range(depth):
        j = i // 2
        if i % 2 == 0:
            mixed = _rwkv_mix(h.reshape(B, T,D), rwkv_mu[j]).reshape(-1, M, D)
            r, k, v = (_matmul(mixed, rwkv_w_rkv, a_prefix=(n,), w_prefix=(j, n)) for n in range(3))
            wl = _lora(mixed, rwkv_w1[j].astype(BF16), rwkv_w2[j].astype(BF16), a_prefix=(3,), act="tanh")
            al = _lora(mixed, rwkv_a1[j].astype(BF16), rwkv_a2[j].astype(BF16), a_prefix=(4,), act="none")
            g = _lora(mixed, rwkv_g1[j].astype(BF16), rwkv_g2[j].astype(BF16), a_prefix=(5,), act="sigmoid")
            r, k, v, wl, al, g = (z.reshape(B, T,D) for z in (r, k, v, wl, al, g))
            y = _rwkv_scan(r, k, v, wl, al, g, rwkv_w0[j], rwkv_a0[j], rwkv_k_k[j], rwkv_k_a[j],
                           rwkv_r_k[j], rwkv_gn_g[j], rwkv_gn_b[j])
            mix = _matmul(y.reshape(M, D), rwkv_w_o, w_prefix=(j,))
        else:
            qkv = _matmul(hb, sb_w_qkv, w_prefix=(j,), out_dtype=BF16)
            o = _sb_attention(qkv.reshape(B, T,3 * D), D // SB_HEAD)
            mix = _matmul(o.reshape(M, D), sb_w_o, w_prefix=(j,))
        h, hb = _add_ln(h, mix, ln_mix_g[i], ln_mix_b[i])
        up = _matmul(hb, w_up, w_prefix=(i,), act="relu2", out_dtype=BF16)
        down = _matmul(up, w_down, w_prefix=(i,), tk=2048, tn=1024)
        h, hb = _add_ln(h, down, ln_ffn_g[i], ln_ffn_b[i])
    return h.reshape(B, T,D)[:, N_META:T]
```

```python
import functools

import jax
import jax.numpy as jnp
from jax import lax
from jax.experimental import pallas as pl
from jax.experimental.pallas import tpu as pltpu

F32 = jnp.float32
BF16 = jnp.bfloat16

N_META = 16
RWKV_HEAD = 64
SB_HEAD = 128
LN_EPS = 1e-5
GN_EPS = 64e-5
L2_EPS = 1e-12
DEPTH = 2
DEEPNORM_ALPHA = (2 * DEPTH) ** 0.25

LANES = 128
MXU_COLS = 256
CHUNK = 64
VMEM_LIMIT = 56 * 1024 * 1024


def _cparams(sem):
    return pltpu.CompilerParams(dimension_semantics=sem, vmem_limit_bytes=VMEM_LIMIT)


def _pick(n, prefs):
    for p in prefs:
        if n % p == 0:
            return p
    return n


def _mm_kernel(a_ref, w_ref, *rest, nk, act):
    o_ref = rest[-1]
    res_ref = rest[0] if len(rest) == 2 else None
    if nk == 1:
        acc = jnp.dot(a_ref[...], w_ref[...].astype(BF16), preferred_element_type=F32)
        if act == "relu2":
            acc = jnp.square(jnp.maximum(acc, 0.0))
        if res_ref is not None:
            acc = DEEPNORM_ALPHA * res_ref[...] + acc
        o_ref[...] = acc.astype(o_ref.dtype)
        return

    tn = o_ref.shape[1]
    cw = _pick(tn, (MXU_COLS, LANES))

    def chunks(first):
        a = a_ref[...]
        for c in range(tn // cw):
            cols = slice(c * cw, (c + 1) * cw)
            part = jnp.dot(a, w_ref[:, cols].astype(BF16), preferred_element_type=F32)
            if not first:
                part = o_ref[:, cols] + part
            elif res_ref is not None:
                part = DEEPNORM_ALPHA * res_ref[:, cols] + part
            o_ref[:, cols] = part

    k = pl.program_id(2)
    pl.when(k == 0)(functools.partial(chunks, True))
    pl.when(k > 0)(functools.partial(chunks, False))


def _matmul(a, w, *, a_prefix=(), w_prefix=(), act="none", out_dtype=F32, residual=None,
            tm=None, tn=None, tk=None):
    M, K = a.shape[-2:]
    N = w.shape[-1]
    tm = _pick(M, (tm or 1376, 1088, 1024, 512, 256, 128, 64, 32, 16))
    tn = _pick(N, (tn or 512, 512, 256, 128))
    tk = _pick(K, (tk or 4096, 2048, 1024, 512, 256, 128))
    nk = K // tk
    na, nw = len(a_prefix), len(w_prefix)
    a_spec = pl.BlockSpec((None,) * na + (tm, tk), lambda i, j, k: a_prefix + (i, k))
    w_spec = pl.BlockSpec((None,) * nw + (tk, tn), lambda i, j, k: w_prefix + (k, j))
    assert nk == 1 or (act == "none" and out_dtype == F32), "K-tiled path accumulates in the f32 output"
    o_spec = pl.BlockSpec((tm, tn), lambda i, j, k: (i, j))
    extra = () if residual is None else (residual,)
    return pl.pallas_call(
        functools.partial(_mm_kernel, nk=nk, act=act),
        out_shape=jax.ShapeDtypeStruct((M, N), out_dtype),
        grid=(M // tm, N // tn, nk),
        in_specs=[a_spec, w_spec] + [o_spec] * len(extra),
        out_specs=o_spec,
        compiler_params=_cparams(("parallel", "parallel", "arbitrary")),
        name="matmul_" + act + ("_res" if extra else ""),
    )(a, w, *extra)


def _lora_kernel(a_ref, w1_ref, w2_ref, o_ref, *, act):
    t = jnp.dot(a_ref[...], w1_ref[...], preferred_element_type=F32)
    if act == "tanh":
        t = jnp.tanh(t)
    elif act == "sigmoid":
        t = 1.0 / (1.0 + jnp.exp(-t))
    o_ref[...] = jnp.dot(t.astype(BF16), w2_ref[...], preferred_element_type=F32)


def _lora(a, w1, w2, *, a_prefix, act):
    M, K = a.shape[-2:]
    R = w1.shape[-1]
    N = w2.shape[-1]
    tm = _pick(M, (688, 544, 512, 256, 128, 64, 32, 16))
    na = len(a_prefix)
    return pl.pallas_call(
        functools.partial(_lora_kernel, act=act),
        out_shape=jax.ShapeDtypeStruct((M, N), F32),
        grid=(M // tm,),
        in_specs=[
            pl.BlockSpec((None,) * na + (tm, K), lambda i: a_prefix + (i, 0)),
            pl.BlockSpec((K, R), lambda i: (0, 0)),
            pl.BlockSpec((R, N), lambda i: (0, 0)),
        ],
        out_specs=pl.BlockSpec((tm, N), lambda i: (i, 0)),
        compiler_params=_cparams(("parallel",)),
        name="lora_" + act,
    )(a, w1, w2)


def _mix_kernel(x_ref, xp_ref, mu_ref, o_ref):
    x = x_ref[0]
    tt = x.shape[0]
    first = pl.program_id(1) == 0
    prev_row = jnp.where(first, 0.0, xp_ref[0, 7:8, :])
    rolled = pltpu.roll(x, 1, 0)
    row = lax.broadcasted_iota(jnp.int32, (tt, 1), 0)
    xprev = jnp.where(row == 0, prev_row, rolled)
    xx = xprev - x
    for n in range(o_ref.shape[0]):
        o_ref[n, 0] = (x + xx * mu_ref[n:n + 1, :]).astype(o_ref.dtype)


def _rwkv_mix(h, mu):
    B, T, D = h.shape
    n_mu = mu.shape[0]
    tt = _pick(T, (688, 256, 128, 64, 32, 16))
    dc = _pick(D, (1024, 512, 256, 128))
    sub = tt // 8
    return pl.pallas_call(
        _mix_kernel,
        out_shape=jax.ShapeDtypeStruct((n_mu, B, T, D), BF16),
        grid=(B, T // tt, D // dc),
        in_specs=[
            pl.BlockSpec((1, tt, dc), lambda b, i, d: (b, i, d)),
            pl.BlockSpec((1, 8, dc), lambda b, i, d: (b, jnp.maximum(i * sub - 1, 0), d)),
            pl.BlockSpec((n_mu, dc), lambda b, i, d: (0, d)),
        ],
        out_specs=pl.BlockSpec((n_mu, 1, tt, dc), lambda b, i, d: (0, b, i, d)),
        compiler_params=_cparams(("parallel", "parallel", "parallel")),
        name="rwkv_mix",
    )(h, h, mu)


def _ln_kernel(z_ref, g_ref, b_ref, o_ref, ob_ref):
    z = z_ref[...]
    mu = jnp.mean(z, axis=-1, keepdims=True)
    zc = z - mu
    var = jnp.mean(jnp.square(zc), axis=-1, keepdims=True)
    out = zc * lax.rsqrt(var + LN_EPS) * g_ref[...] + b_ref[...]
    o_ref[...] = out
    ob_ref[...] = out.astype(BF16)


def _layernorm(z, g, b):
    M, D = z.shape
    tm = _pick(M, (272, 256, 192, 128, 64, 32, 16, 8))
    row = pl.BlockSpec((tm, D), lambda i: (i, 0))
    vec = pl.BlockSpec((1, D), lambda i: (0, 0))
    return pl.pallas_call(
        _ln_kernel,
        out_shape=(jax.ShapeDtypeStruct((M, D), F32), jax.ShapeDtypeStruct((M, D), BF16)),
        grid=(M // tm,),
        in_specs=[row, vec, vec],
        out_specs=(row, row),
        compiler_params=_cparams(("parallel",)),
        name="layernorm",
    )(z, g.reshape(1, D), b.reshape(1, D))


def _split3(x):
    hi = x.astype(BF16)
    r1 = x - hi.astype(F32)
    mid = r1.astype(BF16)
    lo = (r1 - mid.astype(F32)).astype(BF16)
    return hi, mid, lo


def _split2(x):
    hi = x.astype(BF16)
    lo = (x - hi.astype(F32)).astype(BF16)
    return hi, lo


def _dot(a, b):
    return jnp.dot(a, b, preferred_element_type=F32)


def _dot_nt(a, b):
    return lax.dot_general(a, b, (((1,), (1,)), ((), ())), preferred_element_type=F32)


def _dot_tn(a, b):
    return lax.dot_general(a, b, (((0,), (0,)), ((), ())), preferred_element_type=F32)


def _scan_kernel(r_ref, k_ref, v_ref, wl_ref, al_ref, g_ref,
                 w0_ref, a0_ref, kk_ref, ka_ref, rk_ref, gg_ref, gb_ref,
                 o_ref, s_ref, *, seq_len):
    L = CHUNK
    hb = r_ref.shape[-1]
    n_pairs = hb // LANES
    P = 2 * L

    @pl.when(pl.program_id(2) == 0)
    def _():
        s_ref[...] = jnp.zeros_like(s_ref)

    if seq_len % L:
        row = pl.program_id(2) * L + lax.broadcasted_iota(jnp.int32, (L, 1), 0)
        load = lambda ref: jnp.where(row < seq_len, ref[0], 0.0)
    else:
        load = lambda ref: ref[0]
    r = load(r_ref)
    k = load(k_ref)
    v = load(v_ref)

    wpre = w0_ref[...] + load(wl_ref)
    nsp = jnp.maximum(-wpre, 0.0) + jnp.log1p(jnp.exp(-jnp.abs(wpre)))
    lw = -jnp.exp(-nsp - 0.5)
    a = 1.0 / (1.0 + jnp.exp(-(a0_ref[...] + load(al_ref))))

    li = lax.broadcasted_iota(jnp.int32, (2 * LANES, LANES), 0)
    lj = lax.broadcasted_iota(jnp.int32, (2 * LANES, LANES), 1)
    seg_ones = jnp.where(((li % LANES) // RWKV_HEAD) == (lj // RWKV_HEAD), 1.0, 0.0).astype(BF16)
    ti = lax.broadcasted_iota(jnp.int32, (L, L), 0)
    tj = lax.broadcasted_iota(jnp.int32, (L, L), 1)
    tril_incl = jnp.where(tj <= ti, 1.0, 0.0).astype(BF16)

    def to_rows(x):
        return jnp.concatenate([x[:, p * LANES:(p + 1) * LANES] for p in range(n_pairs)], axis=0)

    def to_lanes(x):
        return jnp.concatenate([x[p * L:(p + 1) * L] for p in range(n_pairs)], axis=1)

    def segsum(x):
        hi, lo = _split2(to_rows(x))
        return to_lanes(_dot(jnp.concatenate([hi, lo], axis=1), seg_ones))

    parts = _split3(lw)
    cs = _dot(tril_incl, jnp.concatenate(parts, axis=1))
    G = cs[:, :hb] + cs[:, hb:2 * hb] + cs[:, 2 * hb:]
    GL = G[L - 1:L, :]
    enG = jnp.exp(-G)
    eGL = jnp.exp(GL)
    eTail = jnp.exp(GL - G)

    kkr = k * kk_ref[...]
    kk = kkr / jnp.maximum(jnp.sqrt(segsum(kkr * kkr)), L2_EPS)
    k2 = k * (1.0 + (a - 1.0) * ka_ref[...])
    b = kk * a
    full = dict(R=r * jnp.exp(G), Q=kk * jnp.exp(G - lw), Kt=k2 * enG, Bt=b * enG,
                Kb=k2 * eTail, Bb=b * eTail, V=v)

    pi = lax.broadcasted_iota(jnp.int32, (P, P), 0)
    pj = lax.broadcasted_iota(jnp.int32, (P, P), 1)
    strict = (pj % L) < (pi % L)
    incl = (pj % L) <= (pi % L)
    eye = jnp.where(pi == pj, 1.0, 0.0)
    lane = lax.broadcasted_iota(jnp.int32, (L, LANES), 1)
    head0 = lane < RWKV_HEAD

    def stack(x):
        return jnp.concatenate([jnp.where(head0, x, 0.0), jnp.where(head0, 0.0, x)], axis=0).astype(BF16)

    pairs = range(n_pairs)
    st = {name: [stack(x[:, p * LANES:(p + 1) * LANES]) for p in pairs] for name, x in full.items()}
    S = [s_ref[p] for p in pairs]

    AA = [_dot_nt(jnp.concatenate([st["Q"][p], st["R"][p]], axis=0),
                  jnp.concatenate([st["Bt"][p], st["Kt"][p]], axis=0)) for p in pairs]
    A_qk = [jnp.where(strict, AA[p][:P, P:], 0.0).astype(BF16) for p in pairs]
    A_r = [jnp.concatenate([jnp.where(incl, AA[p][P:, P:], 0.0).astype(BF16),
                            jnp.where(incl, -AA[p][P:, :P], 0.0).astype(BF16)], axis=1) for p in pairs]
    AV = [_dot(A_qk[p], st["V"][p]) for p in pairs]

    Npow = [jnp.where(strict, -AA[p][:P, :P], 0.0) for p in pairs]
    T = [eye + Npow[p] for p in pairs]
    for _ in range((L - 1).bit_length() - 1):
        nb = [Npow[p].astype(BF16) for p in pairs]
        Npow = [_dot(nb[p], nb[p]) for p in pairs]
        T = [T[p] + _dot(T[p].astype(BF16), Npow[p].astype(BF16)) for p in pairs]

    WU = [_dot(T[p].astype(BF16), jnp.concatenate([st["Q"][p], AV[p].astype(BF16)], axis=1)) for p in pairs]
    WR = [_dot_nt(jnp.concatenate([WU[p][:, :LANES].astype(BF16), st["R"][p]], axis=0), S[p].astype(BF16))
          for p in pairs]
    Ub = [(WR[p][:P] + WU[p][:, LANES:]).astype(BF16) for p in pairs]
    Ys = [WR[p][P:] + _dot(A_r[p], jnp.concatenate([st["V"][p], Ub[p]], axis=0)) for p in pairs]
    for p in pairs:
        s_ref[p] = S[p] * eGL[:, p * LANES:(p + 1) * LANES] + _dot_tn(
            jnp.concatenate([st["V"][p], -Ub[p]], axis=0), jnp.concatenate([st["Kb"][p], st["Bb"][p]], axis=0))
    y = jnp.concatenate([Ys[p][:L] + Ys[p][L:] for p in pairs], axis=1)

    mu = segsum(y) * (1.0 / RWKV_HEAD)
    yc = y - mu
    var = segsum(yc * yc) * (1.0 / RWKV_HEAD)
    y_n = yc * lax.rsqrt(var + GN_EPS) * gg_ref[...] + gb_ref[...]
    bonus = segsum(r * k2 * rk_ref[...]) * v
    o_ref[0] = ((y_n + bonus) * load(g_ref)).astype(o_ref.dtype)


def _rwkv_scan(r, k, v, wl, al, g, w0, a0, k_k, k_a, r_k, gn_g, gn_b):
    B, T, D = r.shape
    hb = _pick(D, (2048, 1024, 512, 256, 128))
    seq =pl.BlockSpec((1, CHUNK, hb), lambda b, h, c: (b, c, h))
    vec = pl.BlockSpec((1, hb), lambda b, h, c: (0, h))
    vecs = [x.reshape(1, D) for x in (w0, a0, k_k, k_a, r_k, gn_g, gn_b)]
    return pl.pallas_call(
        functools.partial(_scan_kernel, seq_len=T),
        out_shape=jax.ShapeDtypeStruct((B, T, D), BF16),
        grid=(B, D // hb, pl.cdiv(T, CHUNK)),
        in_specs=[seq] * 6 + [vec] * 7,
        out_specs=seq,
        scratch_shapes=[pltpu.VMEM((hb // LANES, LANES, LANES), F32)],
        compiler_params=_cparams(("parallel", "parallel", "arbitrary")),
        name="rwkv7_scan",
    )(r, k, v, wl, al, g, *vecs)


def _sb_kernel(q_ref, k_ref, v_ref, cm_ref, o_ref, c_ref, acc_ref, *, seq_len):
    tq = q_ref.shape[1]
    tk = tq
    n_h = q_ref.shape[2] // SB_HEAD
    heads = range(n_h)
    qi = pl.program_id(2)
    scale = SB_HEAD ** -0.5
    q = [q_ref[0, :, g * SB_HEAD:(g + 1) * SB_HEAD] for g in heads]
    cm = cm_ref[...]

    def step(kb, nb, diag, carry, tail=0):
        w = nb * tk
        if tail:
            def load(ref, cols):
                x = ref[0, kb * tk:kb * tk + tail, cols]
                return jnp.concatenate([x, jnp.zeros((tk - tail, SB_HEAD), x.dtype)], axis=0)
        else:
            rows = pl.ds(pl.multiple_of(kb * tk, tk), w)
            load = lambda ref, cols: ref[0, rows, cols]
        if diag:
            ri = lax.broadcasted_iota(jnp.int32, (tq, w), 0)
            ci = lax.broadcasted_iota(jnp.int32, (tq, w), 1)
            causal = ci < ri + (w - tk)
        zn, lk = [], []
        for g in heads:
            ks = load(k_ref, slice(g * SB_HEAD, (g + 1) * SB_HEAD))
            zng = _dot_nt(q[g], ks) * (-scale)
            neg_abs = lax.bitcast_convert_type(
                lax.bitcast_convert_type(zng, jnp.int32) | jnp.int32(-2 ** 31), F32)
            lkg = jnp.minimum(zng, 0.0) - jnp.log(jnp.maximum(1.0 + jnp.exp(neg_abs), 1.0))
            if diag:
                lkg = jnp.where(causal, lkg, 0.0)
            zn.append(zng)
            lk.extend(lkg[:, j * tk:(j + 1) * tk] for j in range(nb))
        hi, lo = _split2(jnp.concatenate(lk, axis=0))
        sums = _dot(jnp.concatenate([hi, lo], axis=1), cm)
        out = []
        for g in heads:
            c, acc = carry[g]
            sg = [sums[(g * nb + j) * tq:(g * nb + j + 1) * tq] for j in range(nb)]
            cs = [c]
            for j in range(nb - 1, 0, -1):
                cs.insert(0, cs[0] + sg[j][:, tk:])
            logit = jnp.concatenate([sg[j][:, :tk] + cs[j] for j in range(nb)], axis=1) - zn[g]
            pexp = jnp.exp(logit)
            if diag:
                pexp = jnp.where(causal, pexp, 0.0)
            vs = load(v_ref, slice(g * SB_HEAD, (g + 1) * SB_HEAD))
            out.append((cs[0] + sg[0][:, tk:], acc + _dot(pexp.astype(BF16), vs)))
        return tuple(out)

    def save(carry):
        for g in heads:
            c_ref[g], acc_ref[g] = carry[g]

    def restore():
        return tuple((c_ref[g], acc_ref[g]) for g in heads)

    init = tuple((jnp.zeros((tq, tk), F32), jnp.zeros((tq, SB_HEAD), F32)) for _ in heads)
    odd = qi % 2 == 1
    last, tail = divmod(seq_len, tq)
    if tail:
        assert last % 2 == 0, "ragged last block must follow an even number of full blocks"
        pl.when(qi == last)(lambda: save(step(last, 1, True, init, tail=tail)))
        odd, even = odd & (qi != last), ~odd & (qi != last)
    else:
        even = ~odd
    pl.when(odd)(lambda: save(step(qi - 1, 2, True, init)))
    pl.when(even)(lambda: save(step(qi, 1, True, init)))

    n_tiles = qi // 2
    pl.when(n_tiles % 2 == 1)(lambda: save(step(2 * (n_tiles - 1), 2, False, restore())))
    n_trips = n_tiles // 2

    def body(n, carry):
        t = 2 * (n_trips - 1 - n) + 1
        return step(2 * (t - 1), 2, False, step(2 * t, 2, False, carry))

    carry = lax.fori_loop(0, n_trips, body, restore())
    o_ref[0] = jnp.concatenate([carry[g][1] for g in heads], axis=1).astype(o_ref.dtype)


def _sb_attention(qkv, n_heads):
    B, T, _ = qkv.shape
    tq = SB_HEAD
    gh = _pick(n_heads, (8, 4, 2, 1))
    ng = n_heads // gh
    ri = jnp.arange(2 * tq)[:, None] % tq
    ci = jnp.arange(2 * tq)[None, :]
    cm = jnp.where(ci < tq, ri >= ci, True).astype(BF16)
    return pl.pallas_call(
        functools.partial(_sb_kernel, seq_len=T),
        out_shape=jax.ShapeDtypeStruct((B, T, n_heads * SB_HEAD), BF16),
        grid=(B, ng, pl.cdiv(T, tq)),
        in_specs=[
            pl.BlockSpec((1, tq, gh * SB_HEAD), lambda b, h, i: (b, i, h)),
            pl.BlockSpec((1, T, gh * SB_HEAD), lambda b, h, i: (b, 0, ng + h)),
            pl.BlockSpec((1, T, gh * SB_HEAD), lambda b, h, i: (b, 0, 2 * ng + h)),
            pl.BlockSpec((2 * tq, 2 * tq), lambda b, h, i: (0, 0)),
        ],
        out_specs=pl.BlockSpec((1, tq, gh * SB_HEAD), lambda b, h, i: (b, i, h)),
        scratch_shapes=[pltpu.VMEM((gh, tq, tq), F32), pltpu.VMEM((gh, tq, SB_HEAD), F32)],
        compiler_params=_cparams(("parallel", "parallel", "arbitrary")),
        name="stick_breaking_attention",
    )(qkv, qkv, qkv, cm)


def kernel(x, meta_tokens, ln_mix_g, ln_mix_b, ln_ffn_g, ln_ffn_b, w_up, w_down, rwkv_mu, rwkv_w_rkv, rwkv_w0, rwkv_w1, rwkv_w2, rwkv_a0, rwkv_a1, rwkv_a2, rwkv_g1, rwkv_g2, rwkv_k_k, rwkv_k_a, rwkv_r_k, rwkv_gn_g, rwkv_gn_b, rwkv_w_o, sb_w_qkv, sb_w_o):
    B, seq_len, D = x.shape
    depth = w_up.shape[0]
    T = N_META + seq_len
    M = B * T
    meta = jnp.broadcast_to(meta_tokens[None].astype(x.dtype), (B, N_META, D))
    h = jnp.concatenate([meta, x], axis=1).reshape(M, D)
    hb = None
    for i in range(depth):
        j = i // 2
        if i % 2 == 0:
            mixed = _rwkv_mix(h.reshape(B, T, D), rwkv_mu[j]).reshape(-1, M, D)
            r, k, v = (_matmul(mixed, rwkv_w_rkv, a_prefix=(n,), w_prefix=(j, n)) for n in range(3))
            wl = _lora(mixed, rwkv_w1[j].astype(BF16), rwkv_w2[j].astype(BF16), a_prefix=(3,), act="tanh")
            al = _lora(mixed, rwkv_a1[j].astype(BF16), rwkv_a2[j].astype(BF16), a_prefix=(4,), act="none")
            g = _lora(mixed, rwkv_g1[j].astype(BF16), rwkv_g2[j].astype(BF16), a_prefix=(5,), act="sigmoid")
            r, k, v, wl, al, g = (z.reshape(B, T, D) for z in (r, k, v, wl, al, g))
            y = _rwkv_scan(r, k, v, wl, al, g, rwkv_w0[j], rwkv_a0[j], rwkv_k_k[j], rwkv_k_a[j],
                           rwkv_r_k[j], rwkv_gn_g[j], rwkv_gn_b[j])
            z = _matmul(y.reshape(M, D), rwkv_w_o, w_prefix=(j,), residual=h, tk=2048, tn=1024)
        else:
            qkv = _matmul(hb, sb_w_qkv, w_prefix=(j,), out_dtype=BF16)
            o = _sb_attention(qkv.reshape(B, T, 3 * D), D // SB_HEAD)
            z = _matmul(o.reshape(M, D), sb_w_o, w_prefix=(j,), residual=h, tk=2048, tn=1024)
        h, hb = _layernorm(z, ln_mix_g[i], ln_mix_b[i])
        up = _matmul(hb, w_up, w_prefix=(i,), act="relu2", out_dtype=BF16)
        z = _matmul(up, w_down, w_prefix=(i,), residual=h, tk=2048, tn=1024)
        h, hb = _layernorm(z, ln_ffn_g[i], ln_ffn_b[i])
    return h.reshape(B, T, D)[:, N_META:T]
```

```python
import functools

import jax
import jax.numpy as jnp
from jax import lax
from jax.experimental import pallas as pl
from jax.experimental.pallas import tpu as pltpu

F32 = jnp.float32
BF16 = jnp.bfloat16

N_META = 16
RWKV_HEAD = 64
SB_HEAD = 128
LN_EPS = 1e-5
GN_EPS = 64e-5
L2_EPS = 1e-12
DEPTH = 2
DEEPNORM_ALPHA = (2 * DEPTH) ** 0.25

LANES = 128
MXU_COLS = 256
CHUNK = 64
VMEM_LIMIT = 56 * 1024 * 1024


def _cparams(sem):
    return pltpu.CompilerParams(dimension_semantics=sem, vmem_limit_bytes=VMEM_LIMIT)


def _pick(n, prefs):
    for p in prefs:
        if n % p == 0:
            return p
    return n


def _mm_kernel(a_ref, w_ref, *rest, nk, act):
    o_ref = rest[-1]
    res_ref = rest[0] if len(rest) == 2 else None
    if nk == 1:
        acc = jnp.dot(a_ref[...], w_ref[...].astype(BF16), preferred_element_type=F32)
        if act == "relu2":
            acc = jnp.square(jnp.maximum(acc, 0.0))
        if res_ref is not None:
            acc = DEEPNORM_ALPHA * res_ref[...] + acc
        o_ref[...] = acc.astype(o_ref.dtype)
        return

    tn = o_ref.shape[1]
    cw = _pick(tn, (MXU_COLS, LANES))

    def chunks(first):
        a = a_ref[...]
        for c in range(tn // cw):
            cols = slice(c * cw, (c + 1) * cw)
            part = jnp.dot(a, w_ref[:, cols].astype(BF16), preferred_element_type=F32)
            if not first:
                part = o_ref[:, cols] + part
            elif res_ref is not None:
                part = DEEPNORM_ALPHA * res_ref[:, cols] + part
            o_ref[:, cols] = part

    k = pl.program_id(2)
    pl.when(k == 0)(functools.partial(chunks, True))
    pl.when(k > 0)(functools.partial(chunks, False))


def _matmul(a, w, *, a_prefix=(), w_prefix=(), act="none", out_dtype=F32, residual=None,
            tm=None, tn=None, tk=None):
    M, K = a.shape[-2:]
    N = w.shape[-1]
    tm = _pick(M, (tm or 1376, 1088, 1024, 512, 256, 128, 64, 32, 16))
    tn = _pick(N, (tn or 512, 512, 256, 128))
    tk = _pick(K, (tk or 4096, 2048, 1024, 512, 256, 128))
    nk = K // tk
    na, nw = len(a_prefix), len(w_prefix)
    a_spec = pl.BlockSpec((None,) * na + (tm, tk), lambda i, j, k: a_prefix + (i, k))
    w_spec = pl.BlockSpec((None,) * nw + (tk, tn), lambda i, j, k: w_prefix + (k, j))
    assert nk == 1 or (act == "none" and out_dtype == F32), "K-tiled path accumulates in the f32 output"
    o_spec = pl.BlockSpec((tm, tn), lambda i, j, k: (i, j))
    extra = () if residual is None else (residual,)
    return pl.pallas_call(
        functools.partial(_mm_kernel, nk=nk, act=act),
        out_shape=jax.ShapeDtypeStruct((M, N), out_dtype),
        grid=(M // tm, N // tn, nk),
        in_specs=[a_spec, w_spec] + [o_spec] * len(extra),
        out_specs=o_spec,
        compiler_params=_cparams(("parallel", "parallel", "arbitrary")),
        name="matmul_" + act + ("_res" if extra else ""),
    )(a, w, *extra)


def _lora_up_kernel(t_ref, w2_ref, o_ref):
    o_ref[...] = jnp.dot(t_ref[...].astype(BF16), w2_ref[...], preferred_element_type=F32)


def _lora_up(t, w2):
    M, R = t.shape
    N = w2.shape[-1]
    tm = _pick(M, (688, 544, 512, 256, 128, 64, 32, 16))
    return pl.pallas_call(
        _lora_up_kernel,
        out_shape=jax.ShapeDtypeStruct((M, N), F32),
        grid=(M // tm,),
        in_specs=[pl.BlockSpec((tm, R), lambda i: (i, 0)), pl.BlockSpec((R, N), lambda i: (0, 0))],
        out_specs=pl.BlockSpec((tm, N), lambda i: (i, 0)),
        compiler_params=_cparams(("parallel",)),
        name="lora_up",
    )(t, w2)


def _mix_kernel(x_ref, xp_ref, mu_ref, w1_ref, a1_ref, g1_ref, o_ref, tw_ref, ta_ref, tg_ref):
    x = x_ref[0]
    tt = x.shape[0]
    first = pl.program_id(1) == 0
    prev_row = jnp.where(first, 0.0, xp_ref[0, 7:8, :])
    rolled = pltpu.roll(x, 1, 0)
    row = lax.broadcasted_iota(jnp.int32, (tt, 1), 0)
    xprev = jnp.where(row == 0, prev_row, rolled)
    xx = xprev - x
    n_out = o_ref.shape[0]
    for n in range(n_out):
        o_ref[n, 0] = (x + xx * mu_ref[n:n + 1, :]).astype(o_ref.dtype)

    d = pl.program_id(2)
    for n, (w_ref, t_ref) in enumerate(((w1_ref, tw_ref), (a1_ref, ta_ref), (g1_ref, tg_ref))):
        mixed = (x + xx * mu_ref[n_out + n:n_out + n + 1, :]).astype(BF16)
        part = jnp.dot(mixed, w_ref[...], preferred_element_type=F32)

        @pl.when(d == 0)
        def _():
            t_ref[0] = part

        @pl.when(d > 0)
        def _():
            t_ref[0] += part

    @pl.when(d == pl.num_programs(2) - 1)
    def _():
        tw_ref[0] = jnp.tanh(tw_ref[0])
        tg_ref[0] = 1.0 / (1.0 + jnp.exp(-tg_ref[0]))


def _rwkv_mix(h, mu, w1, a1, g1):
    B, T, D = h.shape
    n_mu = mu.shape[0]
    n_out = n_mu - 3
    tt = _pick(T, (688, 256, 128, 64, 32, 16))
    dc = _pick(D, (1024, 512, 256, 128))
    sub = tt // 8
    lora_in = lambda w: pl.BlockSpec((dc, w.shape[-1]), lambda b, i, d: (d, 0))
    lora_out = lambda w: pl.BlockSpec((1, tt, w.shape[-1]), lambda b, i, d: (b, i, 0))
    return pl.pallas_call(
        _mix_kernel,
        out_shape=(jax.ShapeDtypeStruct((n_out, B, T, D), BF16),)
        + tuple(jax.ShapeDtypeStruct((B, T, w.shape[-1]), F32) for w in (w1, a1, g1)),
        grid=(B, T // tt, D // dc),
        in_specs=[
            pl.BlockSpec((1, tt, dc), lambda b, i, d: (b, i, d)),
            pl.BlockSpec((1, 8, dc), lambda b, i, d: (b, jnp.maximum(i * sub - 1, 0), d)),
            pl.BlockSpec((n_mu, dc), lambda b, i, d: (0, d)),
            lora_in(w1), lora_in(a1), lora_in(g1),
        ],
        out_specs=(pl.BlockSpec((n_out, 1, tt, dc), lambda b, i, d: (0, b, i, d)),
                   lora_out(w1), lora_out(a1), lora_out(g1)),
        compiler_params=_cparams(("parallel", "parallel", "arbitrary")),
        name="rwkv_mix",
    )(h, h, mu, w1, a1, g1)


def _ln_kernel(z_ref, g_ref, b_ref, o_ref, ob_ref):
    z = z_ref[...]
    mu = jnp.mean(z, axis=-1, keepdims=True)
    zc = z - mu
    var = jnp.mean(jnp.square(zc), axis=-1, keepdims=True)
    out = zc * lax.rsqrt(var + LN_EPS) * g_ref[...] + b_ref[...]
    o_ref[...] = out
    ob_ref[...] = out.astype(BF16)


def _layernorm(z, g, b):
    M, D = z.shape
    tm = _pick(M, (272, 256, 192, 128, 64, 32, 16, 8))
    row = pl.BlockSpec((tm, D), lambda i: (i, 0))
    vec = pl.BlockSpec((1, D), lambda i: (0, 0))
    return pl.pallas_call(
        _ln_kernel,
        out_shape=(jax.ShapeDtypeStruct((M, D), F32), jax.ShapeDtypeStruct((M, D), BF16)),
        grid=(M // tm,),
        in_specs=[row, vec, vec],
        out_specs=(row, row),
        compiler_params=_cparams(("parallel",)),
        name="layernorm",
    )(z, g.reshape(1, D), b.reshape(1, D))


def _ln_tail_kernel(z_ref, g_ref, b_ref, o_ref):
    z = z_ref[0]
    mu = jnp.mean(z, axis=-1, keepdims=True)
    zc = z - mu
    var = jnp.mean(jnp.square(zc), axis=-1, keepdims=True)
    o_ref[...] = zc * lax.rsqrt(var + LN_EPS) * g_ref[...] + b_ref[...]


def _layernorm_drop_prefix(z, g, b, skip):
    B, T, D = z.shape
    assert skip % 8 == 0, "row offset must stay sublane-aligned"
    tt = _pick(T - skip, (256, 128, 64, 32, 16, 8))
    vec = pl.BlockSpec((1, D), lambda bi, i: (0, 0))
    return pl.pallas_call(
        _ln_tail_kernel,
        out_shape=jax.ShapeDtypeStruct((B, T - skip, D), F32),
        grid=(B, (T - skip) // tt),
        in_specs=[pl.BlockSpec((pl.Element(1), pl.Element(tt), pl.Element(D)),
                               lambda bi, i: (bi, pl.multiple_of(skip + i * tt, 8), 0)), vec, vec],
        out_specs=pl.BlockSpec((None, tt, D), lambda bi, i: (bi, i, 0)),
        compiler_params=_cparams(("parallel", "parallel")),
        name="layernorm_out",
    )(z, g.reshape(1, D), b.reshape(1, D))


def _split3(x):
    hi = x.astype(BF16)
    r1 = x - hi.astype(F32)
    mid = r1.astype(BF16)
    lo = (r1 - mid.astype(F32)).astype(BF16)
    return hi, mid, lo


def _split2(x):
    hi = x.astype(BF16)
    lo = (x - hi.astype(F32)).astype(BF16)
    return hi, lo


def _dot(a, b):
    return jnp.dot(a, b, preferred_element_type=F32)


def _dot_nt(a, b):
    return lax.dot_general(a, b, (((1,), (1,)), ((), ())), preferred_element_type=F32)


def _dot_tn(a, b):
    return lax.dot_general(a, b, (((0,), (0,)), ((), ())), preferred_element_type=F32)


def _scan_kernel(r_ref, k_ref, v_ref, wl_ref, al_ref, g_ref,
                 w0_ref, a0_ref, kk_ref, ka_ref, rk_ref, gg_ref, gb_ref,
                 o_ref, s_ref, *, seq_len):
    L = CHUNK
    hb = r_ref.shape[-1]
    n_pairs = hb // LANES
    P = 2 * L

    @pl.when(pl.program_id(2) == 0)
    def _():
        s_ref[...] = jnp.zeros_like(s_ref)

    if seq_len % L:
        row = pl.program_id(2) * L + lax.broadcasted_iota(jnp.int32, (L, 1), 0)
        load = lambda ref: jnp.where(row < seq_len, ref[0], 0.0)
    else:
        load = lambda ref: ref[0]
    r = load(r_ref)
    k = load(k_ref)
    v = load(v_ref)

    wpre = w0_ref[...] + load(wl_ref)
    nsp = jnp.maximum(-wpre, 0.0) + jnp.log1p(jnp.exp(-jnp.abs(wpre)))
    lw = -jnp.exp(-nsp - 0.5)
    a = 1.0 / (1.0 + jnp.exp(-(a0_ref[...] + load(al_ref))))

    li = lax.broadcasted_iota(jnp.int32, (2 * LANES, LANES), 0)
    lj = lax.broadcasted_iota(jnp.int32, (2 * LANES, LANES), 1)
    seg_ones = jnp.where(((li % LANES) // RWKV_HEAD) == (lj // RWKV_HEAD), 1.0, 0.0).astype(BF16)
    ti = lax.broadcasted_iota(jnp.int32, (L, L), 0)
    tj = lax.broadcasted_iota(jnp.int32, (L, L), 1)
    tril_incl = jnp.where(tj <= ti, 1.0, 0.0).astype(BF16)

    def to_rows(x):
        return jnp.concatenate([x[:, p * LANES:(p + 1) * LANES] for p in range(n_pairs)], axis=0)

    def to_lanes(x):
        return jnp.concatenate([x[p * L:(p + 1) * L] for p in range(n_pairs)], axis=1)

    def segsum(x):
        hi, lo = _split2(to_rows(x))
        return to_lanes(_dot(jnp.concatenate([hi, lo], axis=1), seg_ones))

    parts = _split3(lw)
    cs = _dot(tril_incl, jnp.concatenate(parts, axis=1))
    G = cs[:, :hb] + cs[:, hb:2 * hb] + cs[:, 2 * hb:]
    GL = G[L - 1:L, :]
    enG = jnp.exp(-G)
    eGL = jnp.exp(GL)
    eTail = jnp.exp(GL - G)

    kkr = k * kk_ref[...]
    kk = kkr / jnp.maximum(jnp.sqrt(segsum(kkr * kkr)), L2_EPS)
    k2 = k * (1.0 + (a - 1.0) * ka_ref[...])
    b = kk * a
    full = dict(R=r * jnp.exp(G), Q=kk * jnp.exp(G - lw), Kt=k2 * enG, Bt=b * enG,
                Kb=k2 * eTail, Bb=b * eTail, V=v)

    pi = lax.broadcasted_iota(jnp.int32, (P, P), 0)
    pj = lax.broadcasted_iota(jnp.int32, (P, P), 1)
    strict = (pj % L) < (pi % L)
    incl = (pj % L) <= (pi % L)
    eye = jnp.where(pi == pj, 1.0, 0.0)
    lane = lax.broadcasted_iota(jnp.int32, (L, LANES), 1)
    head0 = lane < RWKV_HEAD

    def stack(x):
        return jnp.concatenate([jnp.where(head0, x, 0.0), jnp.where(head0, 0.0, x)], axis=0).astype(BF16)

    pairs = range(n_pairs)
    st = {name: [stack(x[:, p * LANES:(p + 1) * LANES]) for p in pairs] for name, x in full.items()}
    S = [s_ref[p] for p in pairs]

    AA = [_dot_nt(jnp.concatenate([st["Q"][p], st["R"][p]], axis=0),
                  jnp.concatenate([st["Bt"][p], st["Kt"][p]], axis=0)) for p in pairs]
    A_qk = [jnp.where(strict, AA[p][:P, P:], 0.0).astype(BF16) for p in pairs]
    A_r = [jnp.concatenate([jnp.where(incl, AA[p][P:, P:], 0.0).astype(BF16),
                            jnp.where(incl, -AA[p][P:, :P], 0.0).astype(BF16)], axis=1) for p in pairs]
    AV = [_dot(A_qk[p], st["V"][p]) for p in pairs]

    Npow = [jnp.where(strict, -AA[p][:P, :P], 0.0) for p in pairs]
    T = [eye + Npow[p] for p in pairs]
    for _ in range((L - 1).bit_length() - 1):
        nb = [Npow[p].astype(BF16) for p in pairs]
        Npow = [_dot(nb[p], nb[p]) for p in pairs]
        T = [T[p] + _dot(T[p].astype(BF16), Npow[p].astype(BF16)) for p in pairs]

    WU = [_dot(T[p].astype(BF16), jnp.concatenate([st["Q"][p], AV[p].astype(BF16)], axis=1)) for p in pairs]
    WR = [_dot_nt(jnp.concatenate([WU[p][:, :LANES].astype(BF16), st["R"][p]], axis=0), S[p].astype(BF16))
          for p in pairs]
    Ub = [(WR[p][:P] + WU[p][:, LANES:]).astype(BF16) for p in pairs]
    Ys = [WR[p][P:] + _dot(A_r[p], jnp.concatenate([st["V"][p], Ub[p]], axis=0)) for p in pairs]
    for p in pairs:
        s_ref[p] = S[p] * eGL[:, p * LANES:(p + 1) * LANES] + _dot_tn(
            jnp.concatenate([st["V"][p], -Ub[p]], axis=0), jnp.concatenate([st["Kb"][p], st["Bb"][p]], axis=0))
    y = jnp.concatenate([Ys[p][:L] + Ys[p][L:] for p in pairs], axis=1)

    mu = segsum(y) * (1.0 / RWKV_HEAD)
    yc = y - mu
    var = segsum(yc * yc) * (1.0 / RWKV_HEAD)
    y_n = yc * lax.rsqrt(var + GN_EPS) * gg_ref[...] + gb_ref[...]
    bonus = segsum(r * k2 * rk_ref[...]) * v
    o_ref[0] = ((y_n + bonus) * load(g_ref)).astype(o_ref.dtype)


def _rwkv_scan(r, k, v, wl, al, g, w0, a0, k_k, k_a, r_k, gn_g, gn_b):
    B, T, D = r.shape
    hb = _pick(D, (2048, 1024, 512, 256, 128))
    seq =pl.BlockSpec((1, CHUNK, hb), lambda b, h, c: (b, c, h))
    vec = pl.BlockSpec((1, hb), lambda b, h, c: (0, h))
    vecs = [x.reshape(1, D) for x in (w0, a0, k_k, k_a, r_k, gn_g, gn_b)]
    return pl.pallas_call(
        functools.partial(_scan_kernel, seq_len=T),
        out_shape=jax.ShapeDtypeStruct((B, T, D), BF16),
        grid=(B, D // hb, pl.cdiv(T, CHUNK)),
        in_specs=[seq] * 6 + [vec] * 7,
        out_specs=seq,
        scratch_shapes=[pltpu.VMEM((hb // LANES, LANES, LANES), F32)],
        compiler_params=_cparams(("parallel", "parallel", "arbitrary")),
        name="rwkv7_scan",
    )(r, k, v, wl, al, g, *vecs)


def _sb_kernel(q_ref, k_ref, v_ref, cm_ref, o_ref, c_ref, acc_ref, *, seq_len):
    tq = q_ref.shape[1]
    tk = tq
    n_h = q_ref.shape[2] // SB_HEAD
    heads = range(n_h)
    qi = pl.program_id(2)
    scale = SB_HEAD ** -0.5
    q = [q_ref[0, :, g * SB_HEAD:(g + 1) * SB_HEAD] for g in heads]
    cm = cm_ref[...]

    def step(kb, nb, diag, carry, tail=0):
        w = nb * tk
        if tail:
            def load(ref, cols):
                x = ref[0, kb * tk:kb * tk + tail, cols]
                return jnp.concatenate([x, jnp.zeros((tk - tail, SB_HEAD), x.dtype)], axis=0)
        else:
            rows = pl.ds(pl.multiple_of(kb * tk, tk), w)
            load = lambda ref, cols: ref[0, rows, cols]
        if diag:
            ri = lax.broadcasted_iota(jnp.int32, (tq, w), 0)
            ci = lax.broadcasted_iota(jnp.int32, (tq, w), 1)
            causal = ci < ri + (w - tk)
        zn, lk = [], []
        for g in heads:
            ks = load(k_ref, slice(g * SB_HEAD, (g + 1) * SB_HEAD))
            zng = _dot_nt(q[g], ks) * (-scale)
            neg_abs = lax.bitcast_convert_type(
                lax.bitcast_convert_type(zng, jnp.int32) | jnp.int32(-2 ** 31), F32)
            lkg = jnp.minimum(zng, 0.0) - jnp.log(jnp.maximum(1.0 + jnp.exp(neg_abs), 1.0))
            if diag:
                lkg = jnp.where(causal, lkg, 0.0)
            zn.append(zng)
            lk.extend(lkg[:, j * tk:(j + 1) * tk] for j in range(nb))
        hi, lo = _split2(jnp.concatenate(lk, axis=0))
        sums = _dot(jnp.concatenate([hi, lo], axis=1), cm)
        out = []
        for g in heads:
            c, acc = carry[g]
            sg = [sums[(g * nb + j) * tq:(g * nb + j + 1) * tq] for j in range(nb)]
            cs = [c]
            for j in range(nb - 1, 0, -1):
                cs.insert(0, cs[0] + sg[j][:, tk:])
            logit = jnp.concatenate([sg[j][:, :tk] + cs[j] for j in range(nb)], axis=1) - zn[g]
            pexp = jnp.exp(logit)
            if diag:
                pexp = jnp.where(causal, pexp, 0.0)
            vs = load(v_ref, slice(g * SB_HEAD, (g + 1) * SB_HEAD))
            out.append((cs[0] + sg[0][:, tk:], acc + _dot(pexp.astype(BF16), vs)))
        return tuple(out)

    def save(carry):
        for g in heads:
            c_ref[g], acc_ref[g] = carry[g]

    def restore():
        return tuple((c_ref[g], acc_ref[g]) for g in heads)

    init = tuple((jnp.zeros((tq, tk), F32), jnp.zeros((tq, SB_HEAD), F32)) for _ in heads)
    odd = qi % 2 == 1
    last, tail = divmod(seq_len, tq)
    if tail:
        assert last % 2 == 0, "ragged last block must follow an even number of full blocks"
        pl.when(qi == last)(lambda: save(step(last, 1, True, init, tail=tail)))
        odd, even = odd & (qi != last), ~odd & (qi != last)
    else:
        even = ~odd
    pl.when(odd)(lambda: save(step(qi - 1, 2, True, init)))
    pl.when(even)(lambda: save(step(qi, 1, True, init)))

    n_tiles = qi // 2
    pl.when(n_tiles % 2 == 1)(lambda: save(step(2 * (n_tiles - 1), 2, False, restore())))
    n_trips = n_tiles // 2

    def body(n, carry):
        t = 2 * (n_trips - 1 - n) + 1
        return step(2 * (t - 1), 2, False, step(2 * t, 2, False, carry))

    carry = lax.fori_loop(0, n_trips, body, restore())
    o_ref[0] = jnp.concatenate([carry[g][1] for g in heads], axis=1).astype(o_ref.dtype)


def _sb_attention(qkv, n_heads):
    B, T, _ = qkv.shape
    tq = SB_HEAD
    gh = _pick(n_heads, (8, 4, 2, 1))
    ng = n_heads // gh
    ri = jnp.arange(2 * tq)[:, None] % tq
    ci = jnp.arange(2 * tq)[None, :]
    cm = jnp.where(ci < tq, ri >= ci, True).astype(BF16)
    return pl.pallas_call(
        functools.partial(_sb_kernel, seq_len=T),
        out_shape=jax.ShapeDtypeStruct((B, T, n_heads * SB_HEAD), BF16),
        grid=(B, ng, pl.cdiv(T, tq)),
        in_specs=[
            pl.BlockSpec((1, tq, gh * SB_HEAD), lambda b, h, i: (b, i, h)),
            pl.BlockSpec((1, T, gh * SB_HEAD), lambda b, h, i: (b, 0, ng + h)),
            pl.BlockSpec((1, T, gh * SB_HEAD), lambda b, h, i: (b, 0, 2 * ng + h)),
            pl.BlockSpec((2 * tq, 2 * tq), lambda b, h, i: (0, 0)),
        ],
        out_specs=pl.BlockSpec((1, tq, gh * SB_HEAD), lambda b, h, i: (b, i, h)),
        scratch_shapes=[pltpu.VMEM((gh, tq, tq), F32), pltpu.VMEM((gh, tq, SB_HEAD), F32)],
        compiler_params=_cparams(("parallel", "parallel", "arbitrary")),
        name="stick_breaking_attention",
    )(qkv, qkv, qkv, cm)


def kernel(x, meta_tokens, ln_mix_g, ln_mix_b, ln_ffn_g, ln_ffn_b, w_up, w_down, rwkv_mu, rwkv_w_rkv, rwkv_w0, rwkv_w1, rwkv_w2, rwkv_a0, rwkv_a1, rwkv_a2, rwkv_g1, rwkv_g2, rwkv_k_k, rwkv_k_a, rwkv_r_k, rwkv_gn_g, rwkv_gn_b, rwkv_w_o, sb_w_qkv, sb_w_o):
    B, seq_len, D = x.shape
    depth = w_up.shape[0]
    T = N_META + seq_len
    M = B * T
    meta = jnp.broadcast_to(meta_tokens[None].astype(x.dtype), (B, N_META, D))
    h = jnp.concatenate([meta, x], axis=1).reshape(M, D)
    hb = None
    for i in range(depth):
        j = i // 2
        if i % 2 == 0:
            mixed, tw, ta, tg = _rwkv_mix(h.reshape(B, T, D), rwkv_mu[j], rwkv_w1[j].astype(BF16),
                                          rwkv_a1[j].astype(BF16), rwkv_g1[j].astype(BF16))
            mixed = mixed.reshape(-1, M, D)
            r, k, v = (_matmul(mixed, rwkv_w_rkv, a_prefix=(n,), w_prefix=(j, n)) for n in range(3))
            wl, al, g = (_lora_up(t.reshape(M, -1), w2[j].astype(BF16))
                         for t, w2 in ((tw, rwkv_w2), (ta, rwkv_a2), (tg, rwkv_g2)))
            r, k, v, wl, al, g = (z.reshape(B, T, D) for z in (r, k, v, wl, al, g))
            y = _rwkv_scan(r, k, v, wl, al, g, rwkv_w0[j], rwkv_a0[j], rwkv_k_k[j], rwkv_k_a[j],
                           rwkv_r_k[j], rwkv_gn_g[j], rwkv_gn_b[j])
            z = _matmul(y.reshape(M, D), rwkv_w_o, w_prefix=(j,), residual=h, tk=2048, tn=1024)
        else:
            qkv = _matmul(hb, sb_w_qkv, w_prefix=(j,), out_dtype=BF16)
            o = _sb_attention(qkv.reshape(B, T, 3 * D), D // SB_HEAD)
            z = _matmul(o.reshape(M, D), sb_w_o, w_prefix=(j,), residual=h, tk=2048, tn=1024)
        h, hb = _layernorm(z, ln_mix_g[i], ln_mix_b[i])
        up = _matmul(hb, w_up, w_prefix=(i,), act="relu2", out_dtype=BF16)
        z = _matmul(up, w_down, w_prefix=(i,), residual=h, tk=2048, tn=1024)
        if i < depth - 1:
            h, hb = _layernorm(z, ln_ffn_g[i], ln_ffn_b[i])
    return _layernorm_drop_prefix(z.reshape(B, T, D), ln_ffn_g[depth - 1], ln_ffn_b[depth - 1], N_META)
```

```python
import functools

import jax
import jax.numpy as jnp
from jax import lax
from jax.experimental import pallas as pl
from jax.experimental.pallas import tpu as pltpu

F32 = jnp.float32
BF16 = jnp.bfloat16

N_META = 16
RWKV_HEAD = 64
SB_HEAD = 128
LN_EPS = 1e-5
GN_EPS = 64e-5
L2_EPS = 1e-12
DEPTH = 2
DEEPNORM_ALPHA = (2 * DEPTH) ** 0.25

LANES = 128
MXU_COLS = 256
CHUNK = 64
VMEM_LIMIT = 56 * 1024 * 1024


def _cparams(sem):
    return pltpu.CompilerParams(dimension_semantics=sem, vmem_limit_bytes=VMEM_LIMIT)


def _pick(n, prefs):
    for p in prefs:
        if n % p == 0:
            return p
    return n


def _mm_kernel(a_ref, w_ref, *rest, nk, act):
    o_ref = rest[-1]
    res_ref = rest[0] if len(rest) == 2 else None
    if nk == 1:
        acc = jnp.dot(a_ref[...], w_ref[...].astype(BF16), preferred_element_type=F32)
        if act == "relu2":
            acc = jnp.square(jnp.maximum(acc, 0.0))
        if res_ref is not None:
            acc = DEEPNORM_ALPHA * res_ref[...] + acc
        o_ref[...] = acc.astype(o_ref.dtype)
        return

    tn = o_ref.shape[1]
    cw = _pick(tn, (MXU_COLS, LANES))

    def chunks(first):
        a = a_ref[...]
        for c in range(tn // cw):
            cols = slice(c * cw, (c + 1) * cw)
            part = jnp.dot(a, w_ref[:, cols].astype(BF16), preferred_element_type=F32)
            if not first:
                part = o_ref[:, cols] + part
            elif res_ref is not None:
                part = DEEPNORM_ALPHA * res_ref[:, cols] + part
            o_ref[:, cols] = part

    k = pl.program_id(2)
    pl.when(k == 0)(functools.partial(chunks, True))
    pl.when(k > 0)(functools.partial(chunks, False))


def _matmul(a, w, *, a_prefix=(), w_prefix=(), act="none", out_dtype=F32, residual=None,
            tm=None, tn=None, tk=None):
    M, K = a.shape[-2:]
    N = w.shape[-1]
    tm = _pick(M, (tm or 1376, 1088, 1024, 512, 256, 128, 64, 32, 16))
    tn = _pick(N, (tn or 512, 512, 256, 128))
    tk = _pick(K, (tk or 4096, 2048, 1024, 512, 256, 128))
    nk = K // tk
    na, nw = len(a_prefix), len(w_prefix)
    a_spec = pl.BlockSpec((None,) * na + (tm, tk), lambda i, j, k: a_prefix + (i, k))
    w_spec = pl.BlockSpec((None,) * nw + (tk, tn), lambda i, j, k: w_prefix + (k, j))
    assert nk == 1 or (act == "none" and out_dtype == F32), "K-tiled path accumulates in the f32 output"
    o_spec = pl.BlockSpec((tm, tn), lambda i, j, k: (i, j))
    extra = () if residual is None else (residual,)
    return pl.pallas_call(
        functools.partial(_mm_kernel, nk=nk, act=act),
        out_shape=jax.ShapeDtypeStruct((M, N), out_dtype),
        grid=(M // tm, N // tn, nk),
        in_specs=[a_spec, w_spec] + [o_spec] * len(extra),
        out_specs=o_spec,
        compiler_params=_cparams(("parallel", "parallel", "arbitrary")),
        name="matmul_" + act + ("_res" if extra else ""),
    )(a, w, *extra)


def _lora_up_kernel(t_ref, w2_ref, o_ref):
    o_ref[...] = jnp.dot(t_ref[...].astype(BF16), w2_ref[...], preferred_element_type=F32)


def _lora_up(t, w2):
    M, R = t.shape
    N = w2.shape[-1]
    tm = _pick(M, (688, 544, 512, 256, 128, 64, 32, 16))
    return pl.pallas_call(
        _lora_up_kernel,
        out_shape=jax.ShapeDtypeStruct((M, N), F32),
        grid=(M // tm,),
        in_specs=[pl.BlockSpec((tm, R), lambda i: (i, 0)), pl.BlockSpec((R, N), lambda i: (0, 0))],
        out_specs=pl.BlockSpec((tm, N), lambda i: (i, 0)),
        compiler_params=_cparams(("parallel",)),
        name="lora_up",
    )(t, w2)


def _mix_kernel(x_ref, xp_ref, mu_ref, w1_ref, a1_ref, g1_ref, o_ref, tw_ref, ta_ref, tg_ref):
    x = x_ref[0]
    tt = x.shape[0]
    first = pl.program_id(1) == 0
    prev_row = jnp.where(first, 0.0, xp_ref[0, 7:8, :])
    rolled = pltpu.roll(x, 1, 0)
    row = lax.broadcasted_iota(jnp.int32, (tt, 1), 0)
    xprev = jnp.where(row == 0, prev_row, rolled)
    xx = xprev - x
    n_out = o_ref.shape[0]
    for n in range(n_out):
        o_ref[n, 0] = (x + xx * mu_ref[n:n + 1, :]).astype(o_ref.dtype)

    d = pl.program_id(2)

    @pl.when(d == 0)
    def _():
        for t_ref in (tw_ref, ta_ref, tg_ref):
            t_ref[...] = jnp.zeros_like(t_ref)

    for n, (w_ref, t_ref) in enumerate(((w1_ref, tw_ref), (a1_ref, ta_ref), (g1_ref, tg_ref))):
        mixed = (x + xx * mu_ref[n_out + n:n_out + n + 1, :]).astype(BF16)
        t_ref[0] += jnp.dot(mixed, w_ref[...], preferred_element_type=F32)

    @pl.when(d == pl.num_programs(2) - 1)
    def _():
        tw_ref[0] = jnp.tanh(tw_ref[0])
        tg_ref[0] = 1.0 / (1.0 + jnp.exp(-tg_ref[0]))


def _rwkv_mix(h, mu, w1, a1, g1):
    B, T, D = h.shape
    n_mu = mu.shape[0]
    n_out = n_mu - 3
    tt = _pick(T, (688, 256, 128, 64, 32, 16))
    dc = _pick(D, (1024, 512, 256, 128))
    sub = tt // 8
    lora_in = lambda w: pl.BlockSpec((dc, w.shape[-1]), lambda b, i, d: (d, 0))
    lora_out = lambda w: pl.BlockSpec((1, tt, w.shape[-1]), lambda b, i, d: (b, i, 0))
    return pl.pallas_call(
        _mix_kernel,
        out_shape=(jax.ShapeDtypeStruct((n_out, B, T, D), BF16),)
        + tuple(jax.ShapeDtypeStruct((B, T, w.shape[-1]), F32) for w in (w1, a1, g1)),
        grid=(B, T // tt, D // dc),
        in_specs=[
            pl.BlockSpec((1, tt, dc), lambda b, i, d: (b, i, d)),
            pl.BlockSpec((1, 8, dc), lambda b, i, d: (b, jnp.maximum(i * sub - 1, 0), d)),
            pl.BlockSpec((n_mu, dc), lambda b, i, d: (0, d)),
            lora_in(w1), lora_in(a1), lora_in(g1),
        ],
        out_specs=(pl.BlockSpec((n_out, 1, tt, dc), lambda b, i, d: (0, b, i, d)),
                   lora_out(w1), lora_out(a1), lora_out(g1)),
        compiler_params=_cparams(("parallel", "parallel", "arbitrary")),
        name="rwkv_mix",
    )(h, h, mu, w1, a1, g1)


def _ln_kernel(z_ref, g_ref, b_ref, o_ref, ob_ref):
    z = z_ref[...]
    mu = jnp.mean(z, axis=-1, keepdims=True)
    zc = z - mu
    var = jnp.mean(jnp.square(zc), axis=-1, keepdims=True)
    out = zc * lax.rsqrt(var + LN_EPS) * g_ref[...] + b_ref[...]
    o_ref[...] = out
    ob_ref[...] = out.astype(BF16)


def _layernorm(z, g, b):
    M, D = z.shape
    tm = _pick(M, (272, 256, 192, 128, 64, 32, 16, 8))
    row = pl.BlockSpec((tm, D), lambda i: (i, 0))
    vec = pl.BlockSpec((1, D), lambda i: (0, 0))
    return pl.pallas_call(
        _ln_kernel,
        out_shape=(jax.ShapeDtypeStruct((M, D), F32), jax.ShapeDtypeStruct((M, D), BF16)),
        grid=(M // tm,),
        in_specs=[row, vec, vec],
        out_specs=(row, row),
        compiler_params=_cparams(("parallel",)),
        name="layernorm",
    )(z, g.reshape(1, D), b.reshape(1, D))


def _ln_tail_kernel(z_ref, g_ref, b_ref, o_ref):
    z = z_ref[0]
    mu = jnp.mean(z, axis=-1, keepdims=True)
    zc = z - mu
    var = jnp.mean(jnp.square(zc), axis=-1, keepdims=True)
    o_ref[...] = zc * lax.rsqrt(var + LN_EPS) * g_ref[...] + b_ref[...]


def _layernorm_drop_prefix(z, g, b, skip):
    B, T, D = z.shape
    assert skip % 8 == 0, "row offset must stay sublane-aligned"
    tt = _pick(T - skip, (256, 128, 64, 32, 16, 8))
    vec = pl.BlockSpec((1, D), lambda bi, i: (0, 0))
    return pl.pallas_call(
        _ln_tail_kernel,
        out_shape=jax.ShapeDtypeStruct((B, T - skip, D), F32),
        grid=(B, (T - skip) // tt),
        in_specs=[pl.BlockSpec((pl.Element(1), pl.Element(tt), pl.Element(D)),
                               lambda bi, i: (bi, pl.multiple_of(skip + i * tt, 8), 0)), vec, vec],
        out_specs=pl.BlockSpec((None, tt, D), lambda bi, i: (bi, i, 0)),
        compiler_params=_cparams(("parallel", "parallel")),
        name="layernorm_out",
    )(z, g.reshape(1, D), b.reshape(1, D))


def _split3(x):
    hi = x.astype(BF16)
    r1 = x - hi.astype(F32)
    mid = r1.astype(BF16)
    lo = (r1 - mid.astype(F32)).astype(BF16)
    return hi, mid, lo


def _split2(x):
    hi = x.astype(BF16)
    lo = (x - hi.astype(F32)).astype(BF16)
    return hi, lo


def _dot(a, b):
    return jnp.dot(a, b, preferred_element_type=F32)


def _dot_nt(a, b):
    return lax.dot_general(a, b, (((1,), (1,)), ((), ())), preferred_element_type=F32)


def _dot_tn(a, b):
    return lax.dot_general(a, b, (((0,), (0,)), ((), ())), preferred_element_type=F32)


def _scan_kernel(r_ref, k_ref, v_ref, wl_ref, al_ref, g_ref,
                 w0_ref, a0_ref, kk_ref, ka_ref, rk_ref, gg_ref, gb_ref,
                 o_ref, s_ref, *, seq_len):
    L = CHUNK
    hb = r_ref.shape[-1]
    n_pairs = hb // LANES
    P = 2 * L

    @pl.when(pl.program_id(2) == 0)
    def _():
        s_ref[...] = jnp.zeros_like(s_ref)

    if seq_len % L:
        row = pl.program_id(2) * L + lax.broadcasted_iota(jnp.int32, (L, 1), 0)
        load = lambda ref: jnp.where(row < seq_len, ref[0], 0.0)
    else:
        load = lambda ref: ref[0]
    r = load(r_ref)
    k = load(k_ref)
    v = load(v_ref)

    wpre = w0_ref[...] + load(wl_ref)
    nsp = jnp.maximum(-wpre, 0.0) + jnp.log1p(jnp.exp(-jnp.abs(wpre)))
    lw = -jnp.exp(-nsp - 0.5)
    a = 1.0 / (1.0 + jnp.exp(-(a0_ref[...] + load(al_ref))))

    li = lax.broadcasted_iota(jnp.int32, (2 * LANES, LANES), 0)
    lj = lax.broadcasted_iota(jnp.int32, (2 * LANES, LANES), 1)
    seg_ones = jnp.where(((li % LANES) // RWKV_HEAD) == (lj // RWKV_HEAD), 1.0, 0.0).astype(BF16)
    ti = lax.broadcasted_iota(jnp.int32, (L, L), 0)
    tj = lax.broadcasted_iota(jnp.int32, (L, L), 1)
    tril_incl = jnp.where(tj <= ti, 1.0, 0.0).astype(BF16)

    def to_rows(x):
        return jnp.concatenate([x[:, p * LANES:(p + 1) * LANES] for p in range(n_pairs)], axis=0)

    def to_lanes(x):
        return jnp.concatenate([x[p * L:(p + 1) * L] for p in range(n_pairs)], axis=1)

    def segsum(x):
        hi, lo = _split2(to_rows(x))
        return to_lanes(_dot(jnp.concatenate([hi, lo], axis=1), seg_ones))

    parts = _split3(lw)
    cs = _dot(tril_incl, jnp.concatenate(parts, axis=1))
    G = cs[:, :hb] + cs[:, hb:2 * hb] + cs[:, 2 * hb:]
    GL = G[L - 1:L, :]
    enG = jnp.exp(-G)
    eGL = jnp.exp(GL)
    eTail = jnp.exp(GL - G)

    kkr = k * kk_ref[...]
    kk = kkr / jnp.maximum(jnp.sqrt(segsum(kkr * kkr)), L2_EPS)
    k2 = k * (1.0 + (a - 1.0) * ka_ref[...])
    b = kk * a
    full = dict(R=r * jnp.exp(G), Q=kk * jnp.exp(G - lw), Kt=k2 * enG, Bt=b * enG,
                Kb=k2 * eTail, Bb=b * eTail, V=v)

    pi = lax.broadcasted_iota(jnp.int32, (P, P), 0)
    pj = lax.broadcasted_iota(jnp.int32, (P, P), 1)
    strict = (pj % L) < (pi % L)
    incl = (pj % L) <= (pi % L)
    eye = jnp.where(pi == pj, 1.0, 0.0)
    lane = lax.broadcasted_iota(jnp.int32, (L, LANES), 1)
    head0 = lane < RWKV_HEAD

    def stack(x):
        return jnp.concatenate([jnp.where(head0, x, 0.0), jnp.where(head0, 0.0, x)], axis=0).astype(BF16)

    pairs = range(n_pairs)
    st = {name: [stack(x[:, p * LANES:(p + 1) * LANES]) for p in pairs] for name, x in full.items()}
    S = [s_ref[p] for p in pairs]

    AA = [_dot_nt(jnp.concatenate([st["Q"][p], st["R"][p]], axis=0),
                  jnp.concatenate([st["Bt"][p], st["Kt"][p]], axis=0)) for p in pairs]
    A_qk = [jnp.where(strict, AA[p][:P, P:], 0.0).astype(BF16) for p in pairs]
    A_r = [jnp.concatenate([jnp.where(incl, AA[p][P:, P:], 0.0).astype(BF16),
                            jnp.where(incl, -AA[p][P:, :P], 0.0).astype(BF16)], axis=1) for p in pairs]
    AV = [_dot(A_qk[p], st["V"][p]) for p in pairs]

    Npow = [jnp.where(strict, -AA[p][:P, :P], 0.0) for p in pairs]
    T = [eye + Npow[p] for p in pairs]
    for _ in range((L - 1).bit_length() - 1):
        nb = [Npow[p].astype(BF16) for p in pairs]
        Npow = [_dot(nb[p], nb[p]) for p in pairs]
        T = [T[p] + _dot(T[p].astype(BF16), Npow[p].astype(BF16)) for p in pairs]

    WU = [_dot(T[p].astype(BF16), jnp.concatenate([st["Q"][p], AV[p].astype(BF16)], axis=1)) for p in pairs]
    WR = [_dot_nt(jnp.concatenate([WU[p][:, :LANES].astype(BF16), st["R"][p]], axis=0), S[p].astype(BF16))
          for p in pairs]
    Ub = [(WR[p][:P] + WU[p][:, LANES:]).astype(BF16) for p in pairs]
    Ys = [WR[p][P:] + _dot(A_r[p], jnp.concatenate([st["V"][p], Ub[p]], axis=0)) for p in pairs]
    for p in pairs:
        s_ref[p] = S[p] * eGL[:, p * LANES:(p + 1) * LANES] + _dot_tn(
            jnp.concatenate([st["V"][p], -Ub[p]], axis=0), jnp.concatenate([st["Kb"][p], st["Bb"][p]], axis=0))
    y = jnp.concatenate([Ys[p][:L] + Ys[p][L:] for p in pairs], axis=1)

    mu = segsum(y) * (1.0 / RWKV_HEAD)
    yc = y - mu
    var = segsum(yc * yc) * (1.0 / RWKV_HEAD)
    y_n = yc * lax.rsqrt(var + GN_EPS) * gg_ref[...] + gb_ref[...]
    bonus = segsum(r * k2 * rk_ref[...]) * v
    o_ref[0] = ((y_n + bonus) * load(g_ref)).astype(o_ref.dtype)


def _rwkv_scan(r, k, v, wl, al, g, w0, a0, k_k, k_a, r_k, gn_g, gn_b):
    B, T, D = r.shape
    hb = _pick(D, (2048, 1024, 512, 256, 128))
    seq =pl.BlockSpec((1, CHUNK, hb), lambda b, h, c: (b, c, h))
    vec = pl.BlockSpec((1, hb), lambda b, h, c: (0, h))
    vecs = [x.reshape(1, D) for x in (w0, a0, k_k, k_a, r_k, gn_g, gn_b)]
    return pl.pallas_call(
        functools.partial(_scan_kernel, seq_len=T),
        out_shape=jax.ShapeDtypeStruct((B, T, D), BF16),
        grid=(B, D // hb, pl.cdiv(T, CHUNK)),
        in_specs=[seq] * 6 + [vec] * 7,
        out_specs=seq,
        scratch_shapes=[pltpu.VMEM((hb // LANES, LANES, LANES), F32)],
        compiler_params=_cparams(("parallel", "parallel", "arbitrary")),
        name="rwkv7_scan",
    )(r, k, v, wl, al, g, *vecs)


def _sb_kernel(q_ref, k_ref, v_ref, cm_ref, o_ref, c_ref, acc_ref, *, seq_len):
    tq = q_ref.shape[1]
    tk = SB_HEAD
    nbq = tq // tk
    n_h = q_ref.shape[2] // SB_HEAD
    heads = range(n_h)
    qi = pl.program_id(2)
    scale = SB_HEAD ** -0.5
    q = [q_ref[0, :, g * SB_HEAD:(g + 1) * SB_HEAD] for g in heads]
    cm = cm_ref[...]

    def step(kb, nb, diag, carry, tail=0):
        w = nb * tk
        if tail:
            def load(ref, cols):
                x = ref[0, kb * tk:kb * tk + tail, cols]
                return jnp.concatenate([x, jnp.zeros((tk - tail, SB_HEAD), x.dtype)], axis=0)
        else:
            rows = pl.ds(pl.multiple_of(kb * tk, tk), w)
            load = lambda ref, cols: ref[0, rows, cols]
        if diag:
            ri = lax.broadcasted_iota(jnp.int32, (tq, w), 0)
            ci = lax.broadcasted_iota(jnp.int32, (tq, w), 1)
            causal = ci < ri
        zn, lk = [], []
        for g in heads:
            ks = load(k_ref, slice(g * SB_HEAD, (g + 1) * SB_HEAD))
            zng = _dot_nt(q[g], ks) * (-scale)
            neg_abs = lax.bitcast_convert_type(
                lax.bitcast_convert_type(zng, jnp.int32) | jnp.int32(-2 ** 31), F32)
            lkg = jnp.minimum(zng, 0.0) - jnp.log(jnp.maximum(1.0 + jnp.exp(neg_abs), 1.0))
            if diag:
                lkg = jnp.where(causal, lkg, 0.0)
            zn.append(zng)
            lk.extend(lkg[:, j * tk:(j + 1) * tk] for j in range(nb))
        hi, lo = _split2(jnp.concatenate(lk, axis=0))
        sums = _dot(jnp.concatenate([hi, lo], axis=1), cm)
        out = []
        for g in heads:
            c, acc = carry[g]
            sg = [sums[(g * nb + j) * tq:(g * nb + j + 1) * tq] for j in range(nb)]
            cs = [c]
            for j in range(nb - 1, 0, -1):
                cs.insert(0, cs[0] + sg[j][:, tk:])
            logit = jnp.concatenate([sg[j][:, :tk] + cs[j] for j in range(nb)], axis=1) - zn[g]
            pexp = jnp.exp(logit)
            if diag:
                pexp = jnp.where(causal, pexp, 0.0)
            vs = load(v_ref, slice(g * SB_HEAD, (g + 1) * SB_HEAD))
            out.append((cs[0] + sg[0][:, tk:], acc + _dot(pexp.astype(BF16), vs)))
        return tuple(out)

    def save(carry):
        for g in heads:
            c_ref[g], acc_ref[g] = carry[g]

    def restore():
        return tuple((c_ref[g], acc_ref[g]) for g in heads)

    init = tuple((jnp.zeros((tq, tk), F32), jnp.zeros((tq, SB_HEAD), F32)) for _ in heads)
    diag_tile = lambda: save(step(qi * nbq, nbq, True, init))
    last, tail = divmod(seq_len, tq)
    if tail:
        assert tail <= tk, "ragged last query block must fit one key block"
        pl.when(qi == last)(lambda: save(step(last * nbq, 1, True, init, tail=tail)))
        pl.when(qi != last)(diag_tile)
    else:
        diag_tile()

    pl.when(qi % 2 == 1)(lambda: save(step((qi - 1) * nbq, nbq, False, restore())))
    n_trips = qi // 2

    def body(n, carry):
        t = 2 * (n_trips - 1 - n) + 1
        return step((t - 1) * nbq, nbq, False, step(t * nbq, nbq, False, carry))

    carry = lax.fori_loop(0, n_trips, body, restore())
    o_ref[0] = jnp.concatenate([carry[g][1] for g in heads], axis=1).astype(o_ref.dtype)


def _sb_attention(qkv, n_heads):
    B, T, _ = qkv.shape
    tk = SB_HEAD
    tq = 2 * tk
    gh = _pick(n_heads, (8, 4, 2, 1))
    ng = n_heads // gh
    ri = jnp.arange(2 * tk)[:, None] % tk
    ci = jnp.arange(2 * tk)[None, :]
    cm = jnp.where(ci < tk, ri >= ci, True).astype(BF16)
    return pl.pallas_call(
        functools.partial(_sb_kernel, seq_len=T),
        out_shape=jax.ShapeDtypeStruct((B, T, n_heads * SB_HEAD), BF16),
        grid=(B, ng, pl.cdiv(T, tq)),
        in_specs=[
            pl.BlockSpec((1, tq, gh * SB_HEAD), lambda b, h, i: (b, i, h)),
            pl.BlockSpec((1, T, gh * SB_HEAD), lambda b, h, i: (b, 0, ng + h)),
            pl.BlockSpec((1, T, gh * SB_HEAD), lambda b, h, i: (b, 0, 2 * ng + h)),
            pl.BlockSpec((2 * tk, 2 * tk), lambda b, h, i: (0, 0)),
        ],
        out_specs=pl.BlockSpec((1, tq, gh * SB_HEAD), lambda b, h, i: (b, i, h)),
        scratch_shapes=[pltpu.VMEM((gh, tq, tk), F32), pltpu.VMEM((gh, tq, SB_HEAD), F32)],
        compiler_params=_cparams(("parallel", "parallel", "arbitrary")),
        name="stick_breaking_attention",
    )(qkv, qkv, qkv, cm)


def kernel(x, meta_tokens, ln_mix_g, ln_mix_b, ln_ffn_g, ln_ffn_b, w_up, w_down, rwkv_mu, rwkv_w_rkv, rwkv_w0, rwkv_w1, rwkv_w2, rwkv_a0, rwkv_a1, rwkv_a2, rwkv_g1, rwkv_g2, rwkv_k_k, rwkv_k_a, rwkv_r_k, rwkv_gn_g, rwkv_gn_b, rwkv_w_o, sb_w_qkv, sb_w_o):
    B, seq_len, D = x.shape
    depth = w_up.shape[0]
    T = N_META + seq_len
    M = B * T
    meta = jnp.broadcast_to(meta_tokens[None].astype(x.dtype), (B, N_META, D))
    h = jnp.concatenate([meta, x], axis=1).reshape(M, D)
    hb = None
    for i in range(depth):
        j = i // 2
        if i % 2 == 0:
            mixed, tw, ta, tg = _rwkv_mix(h.reshape(B, T, D), rwkv_mu[j], rwkv_w1[j].astype(BF16),
                                          rwkv_a1[j].astype(BF16), rwkv_g1[j].astype(BF16))
            mixed = mixed.reshape(-1, M, D)
            r, k, v = (_matmul(mixed, rwkv_w_rkv, a_prefix=(n,), w_prefix=(j, n)) for n in range(3))
            wl, al, g = (_lora_up(t.reshape(M, -1), w2[j].astype(BF16))
                         for t, w2 in ((tw, rwkv_w2), (ta, rwkv_a2), (tg, rwkv_g2)))
            r, k, v, wl, al, g = (z.reshape(B, T, D) for z in (r, k, v, wl, al, g))
            y = _rwkv_scan(r, k, v, wl, al, g, rwkv_w0[j], rwkv_a0[j], rwkv_k_k[j], rwkv_k_a[j],
                           rwkv_r_k[j], rwkv_gn_g[j], rwkv_gn_b[j])
            z = _matmul(y.reshape(M, D), rwkv_w_o, w_prefix=(j,), residual=h, tk=2048, tn=1024)
        else:
            qkv = _matmul(hb, sb_w_qkv, w_prefix=(j,), out_dtype=BF16)
            o = _sb_attention(qkv.reshape(B, T, 3 * D), D // SB_HEAD)
            z = _matmul(o.reshape(M, D), sb_w_o, w_prefix=(j,), residual=h, tk=2048, tn=1024)
        h, hb = _layernorm(z, ln_mix_g[i], ln_mix_b[i])
        up = _matmul(hb, w_up, w_prefix=(i,), act="relu2", out_dtype=BF16)
        z = _matmul(up, w_down, w_prefix=(i,), residual=h, tk=2048, tn=1024)
        if i < depth - 1:
            h, hb = _layernorm(z, ln_ffn_g[i], ln_ffn_b[i])
    return _layernorm_drop_prefix(z.reshape(B, T, D), ln_ffn_g[depth - 1], ln_ffn_b[depth - 1], N_META)
```

```python
import functools

import jax
import jax.numpy as jnp
from jax import lax
from jax.experimental import pallas as pl
from jax.experimental.pallas import tpu as pltpu

F32 = jnp.float32
BF16 = jnp.bfloat16

N_META = 16
RWKV_HEAD = 64
SB_HEAD = 128
LN_EPS = 1e-5
GN_EPS = 64e-5
L2_EPS = 1e-12
DEPTH = 2
DEEPNORM_ALPHA = (2 * DEPTH) ** 0.25

LANES = 128
MXU_COLS = 256
CHUNK = 64
VMEM_LIMIT = 56 * 1024 * 1024
BF16_ROWS = 16

MM_ROW_CAP = 1408
LORA_ROW_CAP = 704
LN_ROW_CAP = 272


def _cparams(sem):
    return pltpu.CompilerParams(dimension_semantics=sem, vmem_limit_bytes=VMEM_LIMIT)


def _pick(n, prefs):
    for p in prefs:
        if n % p == 0:
            return p
    return n


def _row_tile(n, cap, align=BF16_ROWS):
    fits = [t for t in range(align, min(n, cap) + 1, align) if n % t == 0]
    return fits[-1] if fits else n


def _mm_kernel(a_ref, w_ref, *rest, nk, act):
    o_ref = rest[-1]
    res_ref = rest[0] if len(rest) == 2 else None
    if nk == 1:
        acc = jnp.dot(a_ref[...], w_ref[...].astype(BF16), preferred_element_type=F32)
        if act == "relu2":
            acc = jnp.square(jnp.maximum(acc, 0.0))
        if res_ref is not None:
            acc = DEEPNORM_ALPHA * res_ref[...] + acc
        o_ref[...] = acc.astype(o_ref.dtype)
        return

    tn = o_ref.shape[1]
    cw = _pick(tn, (MXU_COLS, LANES))

    def chunks(first):
        a = a_ref[...]
        for c in range(tn // cw):
            cols = slice(c * cw, (c + 1) * cw)
            part = jnp.dot(a, w_ref[:, cols].astype(BF16), preferred_element_type=F32)
            if not first:
                part = o_ref[:, cols] + part
            elif res_ref is not None:
                part = DEEPNORM_ALPHA * res_ref[:, cols] + part
            o_ref[:, cols] = part

    k = pl.program_id(2)
    pl.when(k == 0)(functools.partial(chunks, True))
    pl.when(k > 0)(functools.partial(chunks, False))


def _matmul(a, w, *, a_prefix=(), w_prefix=(), act="none", out_dtype=F32, residual=None,
            tm=None, tn=None, tk=None):
    M, K = a.shape[-2:]
    N = w.shape[-1]
    tm = tm or _row_tile(M, MM_ROW_CAP)
    tn = _pick(N, (tn or 512, 512, 256, 128))
    tk = _pick(K, (tk or 4096, 2048, 1024, 512, 256, 128))
    nk = K // tk
    na, nw = len(a_prefix), len(w_prefix)
    a_spec = pl.BlockSpec((None,) * na + (tm, tk), lambda i, j, k: a_prefix + (i, k))
    w_spec = pl.BlockSpec((None,) * nw + (tk, tn), lambda i, j, k: w_prefix + (k, j))
    assert nk == 1 or (act == "none" and out_dtype == F32), "K-tiled path accumulates in the f32 output"
    o_spec = pl.BlockSpec((tm, tn), lambda i, j, k: (i, j))
    extra = () if residual is None else (residual,)
    return pl.pallas_call(
        functools.partial(_mm_kernel, nk=nk, act=act),
        out_shape=jax.ShapeDtypeStruct((M, N), out_dtype),
        grid=(M // tm, N // tn, nk),
        in_specs=[a_spec, w_spec] + [o_spec] * len(extra),
        out_specs=o_spec,
        compiler_params=_cparams(("parallel", "parallel", "arbitrary")),
        name="matmul_" + act + ("_res" if extra else ""),
    )(a, w, *extra)


def _lora_up_kernel(t_ref, w2_ref, o_ref):
    o_ref[...] = jnp.dot(t_ref[...].astype(BF16), w2_ref[...], preferred_element_type=F32)


def _lora_up(t, w2):
    M, R = t.shape
    N = w2.shape[-1]
    tm = _row_tile(M, LORA_ROW_CAP)
    return pl.pallas_call(
        _lora_up_kernel,
        out_shape=jax.ShapeDtypeStruct((M, N), F32),
        grid=(M // tm,),
        in_specs=[pl.BlockSpec((tm, R), lambda i: (i, 0)), pl.BlockSpec((R, N), lambda i: (0, 0))],
        out_specs=pl.BlockSpec((tm, N), lambda i: (i, 0)),
        compiler_params=_cparams(("parallel",)),
        name="lora_up",
    )(t, w2)


def _mix_kernel(x_ref, xp_ref, mu_ref, w1_ref, a1_ref, g1_ref, o_ref, tw_ref, ta_ref, tg_ref):
    x = x_ref[0]
    tt = x.shape[0]
    first = pl.program_id(1) == 0
    prev_row = jnp.where(first, 0.0, xp_ref[0, 7:8, :])
    rolled = pltpu.roll(x, 1, 0)
    row = lax.broadcasted_iota(jnp.int32, (tt, 1), 0)
    xprev = jnp.where(row == 0, prev_row, rolled)
    xx = xprev - x
    n_out = o_ref.shape[0]
    for n in range(n_out):
        o_ref[n, 0] = (x + xx * mu_ref[n:n + 1, :]).astype(o_ref.dtype)

    d = pl.program_id(2)

    @pl.when(d == 0)
    def _():
        for t_ref in (tw_ref, ta_ref, tg_ref):
            t_ref[...] = jnp.zeros_like(t_ref)

    for n, (w_ref, t_ref) in enumerate(((w1_ref, tw_ref), (a1_ref, ta_ref), (g1_ref, tg_ref))):
        mixed = (x + xx * mu_ref[n_out + n:n_out + n + 1, :]).astype(BF16)
        t_ref[0] += jnp.dot(mixed, w_ref[...], preferred_element_type=F32)

    @pl.when(d == pl.num_programs(2) - 1)
    def _():
        tw_ref[0] = jnp.tanh(tw_ref[0])
        tg_ref[0] = 1.0 / (1.0 + jnp.exp(-tg_ref[0]))


def _rwkv_mix(h, mu, w1, a1, g1):
    B, T, D = h.shape
    n_mu = mu.shape[0]
    n_out = n_mu - 3
    tt = _row_tile(T, LORA_ROW_CAP)
    dc = _pick(D, (1024, 512, 256, 128))
    sub = tt // 8
    lora_in = lambda w: pl.BlockSpec((dc, w.shape[-1]), lambda b, i, d: (d, 0))
    lora_out = lambda w: pl.BlockSpec((1, tt, w.shape[-1]), lambda b, i, d: (b, i, 0))
    return pl.pallas_call(
        _mix_kernel,
        out_shape=(jax.ShapeDtypeStruct((n_out, B, T, D), BF16),)
        + tuple(jax.ShapeDtypeStruct((B, T, w.shape[-1]), F32) for w in (w1, a1, g1)),
        grid=(B, T // tt, D // dc),
        in_specs=[
            pl.BlockSpec((1, tt, dc), lambda b, i, d: (b, i, d)),
            pl.BlockSpec((1, 8, dc), lambda b, i, d: (b, jnp.maximum(i * sub - 1, 0), d)),
            pl.BlockSpec((n_mu, dc), lambda b, i, d: (0, d)),
            lora_in(w1), lora_in(a1), lora_in(g1),
        ],
        out_specs=(pl.BlockSpec((n_out, 1, tt, dc), lambda b, i, d: (0, b, i, d)),
                   lora_out(w1), lora_out(a1), lora_out(g1)),
        compiler_params=_cparams(("parallel", "parallel", "arbitrary")),
        name="rwkv_mix",
    )(h, h, mu, w1, a1, g1)


def _ln_kernel(z_ref, g_ref, b_ref, o_ref, ob_ref):
    z = z_ref[...]
    mu = jnp.mean(z, axis=-1, keepdims=True)
    zc = z - mu
    var = jnp.mean(jnp.square(zc), axis=-1, keepdims=True)
    out = zc * lax.rsqrt(var + LN_EPS) * g_ref[...] + b_ref[...]
    o_ref[...] = out
    ob_ref[...] = out.astype(BF16)


def _layernorm(z, g, b):
    M, D = z.shape
    tm = _row_tile(M, LN_ROW_CAP)
    row = pl.BlockSpec((tm, D), lambda i: (i, 0))
    vec = pl.BlockSpec((1, D), lambda i: (0, 0))
    return pl.pallas_call(
        _ln_kernel,
        out_shape=(jax.ShapeDtypeStruct((M, D), F32), jax.ShapeDtypeStruct((M, D), BF16)),
        grid=(M // tm,),
        in_specs=[row, vec, vec],
        out_specs=(row, row),
        compiler_params=_cparams(("parallel",)),
        name="layernorm",
    )(z, g.reshape(1, D), b.reshape(1, D))


def _ln_tail_kernel(z_ref, g_ref, b_ref, o_ref):
    z = z_ref[0]
    mu = jnp.mean(z, axis=-1, keepdims=True)
    zc = z - mu
    var = jnp.mean(jnp.square(zc), axis=-1, keepdims=True)
    o_ref[...] = zc * lax.rsqrt(var + LN_EPS) * g_ref[...] + b_ref[...]


def _layernorm_drop_prefix(z, g, b, skip):
    B, T, D = z.shape
    assert skip % 8 == 0, "row offset must stay sublane-aligned"
    tt = _row_tile(T - skip, LN_ROW_CAP, align=8)
    vec = pl.BlockSpec((1, D), lambda bi, i: (0, 0))
    return pl.pallas_call(
        _ln_tail_kernel,
        out_shape=jax.ShapeDtypeStruct((B, T - skip, D), F32),
        grid=(B, (T - skip) // tt),
        in_specs=[pl.BlockSpec((pl.Element(1), pl.Element(tt), pl.Element(D)),
                               lambda bi, i: (bi, pl.multiple_of(skip + i * tt, 8), 0)), vec, vec],
        out_specs=pl.BlockSpec((None, tt, D), lambda bi, i: (bi, i, 0)),
        compiler_params=_cparams(("parallel", "parallel")),
        name="layernorm_out",
    )(z, g.reshape(1, D), b.reshape(1, D))


def _split3(x):
    hi = x.astype(BF16)
    r1 = x - hi.astype(F32)
    mid = r1.astype(BF16)
    lo = (r1 - mid.astype(F32)).astype(BF16)
    return hi, mid, lo


def _split2(x):
    hi = x.astype(BF16)
    lo = (x - hi.astype(F32)).astype(BF16)
    return hi, lo


def _dot(a, b):
    return jnp.dot(a, b, preferred_element_type=F32)


def _dot_nt(a, b):
    return lax.dot_general(a, b, (((1,), (1,)), ((), ())), preferred_element_type=F32)


def _dot_tn(a, b):
    return lax.dot_general(a, b, (((0,), (0,)), ((), ())), preferred_element_type=F32)


def _scan_kernel(r_ref, k_ref, v_ref, wl_ref, al_ref, g_ref,
                 w0_ref, a0_ref, kk_ref, ka_ref, rk_ref, gg_ref, gb_ref,
                 o_ref, s_ref, *, seq_len):
    L = CHUNK
    hb = r_ref.shape[-1]
    n_pairs = hb // LANES
    P = 2 * L

    @pl.when(pl.program_id(2) == 0)
    def _():
        s_ref[...] = jnp.zeros_like(s_ref)

    if seq_len % L:
        row = pl.program_id(2) * L + lax.broadcasted_iota(jnp.int32, (L, 1), 0)
        load = lambda ref: jnp.where(row < seq_len, ref[0], 0.0)
    else:
        load = lambda ref: ref[0]
    r = load(r_ref)
    k = load(k_ref)
    v = load(v_ref)

    wpre = w0_ref[...] + load(wl_ref)
    nsp = jnp.maximum(-wpre, 0.0) + jnp.log1p(jnp.exp(-jnp.abs(wpre)))
    lw = -jnp.exp(-nsp - 0.5)
    a = 1.0 / (1.0 + jnp.exp(-(a0_ref[...] + load(al_ref))))

    li = lax.broadcasted_iota(jnp.int32, (2 * LANES, LANES), 0)
    lj = lax.broadcasted_iota(jnp.int32, (2 * LANES, LANES), 1)
    seg_ones = jnp.where(((li % LANES) // RWKV_HEAD) == (lj // RWKV_HEAD), 1.0, 0.0).astype(BF16)
    ti = lax.broadcasted_iota(jnp.int32, (L, L), 0)
    tj = lax.broadcasted_iota(jnp.int32, (L, L), 1)
    tril_incl = jnp.where(tj <= ti, 1.0, 0.0).astype(BF16)

    def to_rows(x):
        return jnp.concatenate([x[:, p * LANES:(p + 1) * LANES] for p in range(n_pairs)], axis=0)

    def to_lanes(x):
        return jnp.concatenate([x[p * L:(p + 1) * L] for p in range(n_pairs)], axis=1)

    def segsum(x):
        hi, lo = _split2(to_rows(x))
        return to_lanes(_dot(jnp.concatenate([hi, lo], axis=1), seg_ones))

    parts = _split3(lw)
    cs = _dot(tril_incl, jnp.concatenate(parts, axis=1))
    G = cs[:, :hb] + cs[:, hb:2 * hb] + cs[:, 2 * hb:]
    GL = G[L - 1:L, :]
    enG = jnp.exp(-G)
    eGL = jnp.exp(GL)
    eTail = jnp.exp(GL - G)

    kkr = k * kk_ref[...]
    kk = kkr / jnp.maximum(jnp.sqrt(segsum(kkr * kkr)), L2_EPS)
    k2 = k * (1.0 + (a - 1.0) * ka_ref[...])
    b = kk * a
    full = dict(R=r * jnp.exp(G), Q=kk * jnp.exp(G - lw), Kt=k2 * enG, Bt=b * enG,
                Kb=k2 * eTail, Bb=b * eTail, V=v)

    pi = lax.broadcasted_iota(jnp.int32, (P, P), 0)
    pj = lax.broadcasted_iota(jnp.int32, (P, P), 1)
    strict = (pj % L) < (pi % L)
    incl = (pj % L) <= (pi % L)
    eye = jnp.where(pi == pj, 1.0, 0.0)
    lane = lax.broadcasted_iota(jnp.int32, (L, LANES), 1)
    head0 = lane < RWKV_HEAD

    def stack(x):
        return jnp.concatenate([jnp.where(head0, x, 0.0), jnp.where(head0, 0.0, x)], axis=0).astype(BF16)

    pairs = range(n_pairs)
    st = {name: [stack(x[:, p * LANES:(p + 1) * LANES]) for p in pairs] for name, x in full.items()}
    S = [s_ref[p] for p in pairs]

    AA = [_dot_nt(jnp.concatenate([st["Q"][p], st["R"][p]], axis=0),
                  jnp.concatenate([st["Bt"][p], st["Kt"][p]], axis=0)) for p in pairs]
    A_qk = [jnp.where(strict, AA[p][:P, P:], 0.0).astype(BF16) for p in pairs]
    A_r = [jnp.concatenate([jnp.where(incl, AA[p][P:, P:], 0.0).astype(BF16),
                            jnp.where(incl, -AA[p][P:, :P], 0.0).astype(BF16)], axis=1) for p in pairs]
    AV = [_dot(A_qk[p], st["V"][p]) for p in pairs]

    Npow = [jnp.where(strict, -AA[p][:P, :P], 0.0) for p in pairs]
    T = [eye + Npow[p] for p in pairs]
    for _ in range((L - 1).bit_length() - 1):
        nb = [Npow[p].astype(BF16) for p in pairs]
        Npow = [_dot(nb[p], nb[p]) for p in pairs]
        T = [T[p] + _dot(T[p].astype(BF16), Npow[p].astype(BF16)) for p in pairs]

    WU = [_dot(T[p].astype(BF16), jnp.concatenate([st["Q"][p], AV[p].astype(BF16)], axis=1)) for p in pairs]
    WR = [_dot_nt(jnp.concatenate([WU[p][:, :LANES].astype(BF16), st["R"][p]], axis=0), S[p].astype(BF16))
          for p in pairs]
    Ub = [(WR[p][:P] + WU[p][:, LANES:]).astype(BF16) for p in pairs]
    Ys = [WR[p][P:] + _dot(A_r[p], jnp.concatenate([st["V"][p], Ub[p]], axis=0)) for p in pairs]
    for p in pairs:
        s_ref[p] = S[p] * eGL[:, p * LANES:(p + 1) * LANES] + _dot_tn(
            jnp.concatenate([st["V"][p], -Ub[p]], axis=0), jnp.concatenate([st["Kb"][p], st["Bb"][p]], axis=0))
    y = jnp.concatenate([Ys[p][:L] + Ys[p][L:] for p in pairs], axis=1)

    mu = segsum(y) * (1.0 / RWKV_HEAD)
    yc = y - mu
    var = segsum(yc * yc) * (1.0 / RWKV_HEAD)
    y_n = yc * lax.rsqrt(var + GN_EPS) * gg_ref[...] + gb_ref[...]
    bonus = segsum(r * k2 * rk_ref[...]) * v
    o_ref[0] = ((y_n + bonus) * load(g_ref)).astype(o_ref.dtype)


def _rwkv_scan(r, k, v, wl, al, g, w0, a0, k_k, k_a, r_k, gn_g, gn_b):
    B, T, D = r.shape
    hb = _pick(D, (4096, 2048, 1024, 512, 256, 128))
    seq =pl.BlockSpec((1, CHUNK, hb), lambda b, h, c: (b, c, h))
    vec = pl.BlockSpec((1, hb), lambda b, h, c: (0, h))
    vecs = [x.reshape(1, D) for x in (w0, a0, k_k, k_a, r_k, gn_g, gn_b)]
    return pl.pallas_call(
        functools.partial(_scan_kernel, seq_len=T),
        out_shape=jax.ShapeDtypeStruct((B, T, D), BF16),
        grid=(B, D // hb, pl.cdiv(T, CHUNK)),
        in_specs=[seq] * 6 + [vec] * 7,
        out_specs=seq,
        scratch_shapes=[pltpu.VMEM((hb // LANES, LANES, LANES), F32)],
        compiler_params=_cparams(("parallel", "parallel", "arbitrary")),
        name="rwkv7_scan",
    )(r, k, v, wl, al, g, *vecs)


def _sb_kernel(q_ref, k_ref, v_ref, cm_ref, o_ref, c_ref, acc_ref, *, seq_len):
    tq = q_ref.shape[1]
    tk = SB_HEAD
    nbq = tq // tk
    n_h = q_ref.shape[2] // SB_HEAD
    heads = range(n_h)
    qi = pl.program_id(2)
    scale = SB_HEAD ** -0.5
    q = [q_ref[0, :, g * SB_HEAD:(g + 1) * SB_HEAD] for g in heads]
    cm = cm_ref[...]

    def step(kb, nb, diag, carry, tail=0):
        w = nb * tk
        if tail:
            def load(ref, cols):
                x = ref[0, kb * tk:kb * tk + tail, cols]
                return jnp.concatenate([x, jnp.zeros((tk - tail, SB_HEAD), x.dtype)], axis=0)
        else:
            rows = pl.ds(pl.multiple_of(kb * tk, tk), w)
            load = lambda ref, cols: ref[0, rows, cols]
        if diag:
            ri = lax.broadcasted_iota(jnp.int32, (tq, w), 0)
            ci = lax.broadcasted_iota(jnp.int32, (tq, w), 1)
            causal = ci < ri
        zn, lk = [], []
        for g in heads:
            ks = load(k_ref, slice(g * SB_HEAD, (g + 1) * SB_HEAD))
            zng = _dot_nt(q[g], ks) * (-scale)
            neg_abs = lax.bitcast_convert_type(
                lax.bitcast_convert_type(zng, jnp.int32) | jnp.int32(-2 ** 31), F32)
            lkg = jnp.minimum(zng, 0.0) - jnp.log(jnp.maximum(1.0 + jnp.exp(neg_abs), 1.0))
            if diag:
                lkg = jnp.where(causal, lkg, 0.0)
            zn.append(zng)
            lk.extend(lkg[:, j * tk:(j + 1) * tk] for j in range(nb))
        hi, lo = _split2(jnp.concatenate(lk, axis=0))
        sums = _dot(jnp.concatenate([hi, lo], axis=1), cm)
        out = []
        for g in heads:
            c, acc = carry[g]
            sg = [sums[(g * nb + j) * tq:(g * nb + j + 1) * tq] for j in range(nb)]
            cs = [c]
            for j in range(nb - 1, 0, -1):
                cs.insert(0, cs[0] + sg[j][:, tk:])
            logit = jnp.concatenate([sg[j][:, :tk] + cs[j] for j in range(nb)], axis=1) - zn[g]
            pexp = jnp.exp(logit)
            if diag:
                pexp = jnp.where(causal, pexp, 0.0)
            vs = load(v_ref, slice(g * SB_HEAD, (g + 1) * SB_HEAD))
            out.append((cs[0] + sg[0][:, tk:], acc + _dot(pexp.astype(BF16), vs)))
        return tuple(out)

    def save(carry):
        for g in heads:
            c_ref[g], acc_ref[g] = carry[g]

    def restore():
        return tuple((c_ref[g], acc_ref[g]) for g in heads)

    init = tuple((jnp.zeros((tq, tk), F32), jnp.zeros((tq, SB_HEAD), F32)) for _ in heads)
    diag_tile = lambda: save(step(qi * nbq, nbq, True, init))
    last, tail = divmod(seq_len, tq)
    if tail:
        assert tail <= tk, "ragged last query block must fit one key block"
        pl.when(qi == last)(lambda: save(step(last * nbq, 1, True, init, tail=tail)))
        pl.when(qi != last)(diag_tile)
    else:
        diag_tile()

    pl.when(qi % 2 == 1)(lambda: save(step((qi - 1) * nbq, nbq, False, restore())))
    n_trips = qi // 2

    def body(n, carry):
        t = 2 * (n_trips - 1 - n) + 1
        return step((t - 1) * nbq, nbq, False, step(t * nbq, nbq, False, carry))

    carry = lax.fori_loop(0, n_trips, body, restore())
    o_ref[0] = jnp.concatenate([carry[g][1] for g in heads], axis=1).astype(o_ref.dtype)


def _sb_attention(qkv, n_heads):
    B, T, _ = qkv.shape
    tk = SB_HEAD
    tq = 2 * tk
    gh = _pick(n_heads, (8, 4, 2, 1))
    ng = n_heads // gh
    ri = jnp.arange(2 * tk)[:, None] % tk
    ci = jnp.arange(2 * tk)[None, :]
    cm = jnp.where(ci < tk, ri >= ci, True).astype(BF16)
    return pl.pallas_call(
        functools.partial(_sb_kernel, seq_len=T),
        out_shape=jax.ShapeDtypeStruct((B, T, n_heads * SB_HEAD), BF16),
        grid=(B, ng, pl.cdiv(T, tq)),
        in_specs=[
            pl.BlockSpec((1, tq, gh * SB_HEAD), lambda b, h, i: (b, i, h)),
            pl.BlockSpec((1, T, gh * SB_HEAD), lambda b, h, i: (b, 0, ng + h)),
            pl.BlockSpec((1, T, gh * SB_HEAD), lambda b, h, i: (b, 0, 2 * ng + h)),
            pl.BlockSpec((2 * tk, 2 * tk), lambda b, h, i: (0, 0)),
        ],
        out_specs=pl.BlockSpec((1, tq, gh * SB_HEAD), lambda b, h, i: (b, i, h)),
        scratch_shapes=[pltpu.VMEM((gh, tq, tk), F32), pltpu.VMEM((gh, tq, SB_HEAD), F32)],
        compiler_params=_cparams(("parallel", "parallel", "arbitrary")),
        name="stick_breaking_attention",
    )(qkv, qkv, qkv, cm)


def kernel(x, meta_tokens, ln_mix_g, ln_mix_b, ln_ffn_g, ln_ffn_b, w_up, w_down, rwkv_mu, rwkv_w_rkv, rwkv_w0, rwkv_w1, rwkv_w2, rwkv_a0, rwkv_a1, rwkv_a2, rwkv_g1, rwkv_g2, rwkv_k_k, rwkv_k_a, rwkv_r_k, rwkv_gn_g, rwkv_gn_b, rwkv_w_o, sb_w_qkv, sb_w_o):
    B, seq_len, D = x.shape
    depth = w_up.shape[0]
    T = N_META + seq_len
    M = B * T
    meta = jnp.broadcast_to(meta_tokens[None].astype(x.dtype), (B, N_META, D))
    h = jnp.concatenate([meta, x], axis=1).reshape(M, D)
    hb = None
    for i in range(depth):
        j = i // 2
        if i % 2 == 0:
            mixed, tw, ta, tg = _rwkv_mix(h.reshape(B, T, D), rwkv_mu[j], rwkv_w1[j].astype(BF16),
                                          rwkv_a1[j].astype(BF16), rwkv_g1[j].astype(BF16))
            mixed = mixed.reshape(-1, M, D)
            r, k, v = (_matmul(mixed, rwkv_w_rkv, a_prefix=(n,), w_prefix=(j, n)) for n in range(3))
            wl, al, g = (_lora_up(t.reshape(M, -1), w2[j].astype(BF16))
                         for t, w2 in ((tw, rwkv_w2), (ta, rwkv_a2), (tg, rwkv_g2)))
            r, k, v, wl, al, g = (z.reshape(B, T, D) for z in (r, k, v, wl, al, g))
            y = _rwkv_scan(r, k, v, wl, al, g, rwkv_w0[j], rwkv_a0[j], rwkv_k_k[j], rwkv_k_a[j],
                           rwkv_r_k[j], rwkv_gn_g[j], rwkv_gn_b[j])
            z = _matmul(y.reshape(M, D), rwkv_w_o, w_prefix=(j,), residual=h, tk=2048, tn=1024)
        else:
            qkv = _matmul(hb, sb_w_qkv, w_prefix=(j,), out_dtype=BF16)
            o = _sb_attention(qkv.reshape(B, T, 3 * D), D // SB_HEAD)
            z = _matmul(o.reshape(M, D), sb_w_o, w_prefix=(j,), residual=h, tk=2048, tn=1024)
        h, hb = _layernorm(z, ln_mix_g[i], ln_mix_b[i])
        up = _matmul(hb, w_up, w_prefix=(i,), act="relu2", out_dtype=BF16)
        z = _matmul(up, w_down, w_prefix=(i,), residual=h, tk=2048, tn=1024)
        if i < depth - 1:
            h, hb = _layernorm(z, ln_ffn_g[i], ln_ffn_b[i])
    return _layernorm_drop_prefix(z.reshape(B, T, D), ln_ffn_g[depth - 1], ln_ffn_b[depth - 1], N_META)
```

```python
import functools

import jax
import jax.numpy as jnp
from jax import lax
from jax.experimental import pallas as pl
from jax.experimental.pallas import tpu as pltpu

F32 = jnp.float32
BF16 = jnp.bfloat16

N_META = 16
RWKV_HEAD = 64
SB_HEAD = 128
LN_EPS = 1e-5
GN_EPS = 64e-5
L2_EPS = 1e-12
DEPTH = 2
DEEPNORM_ALPHA = (2 * DEPTH) ** 0.25

LANES = 128
SUBLANES = 8
MXU_COLS = 256
CHUNK = 64
VMEM_LIMIT = 56 * 1024 * 1024
BF16_ROWS = 16

MM_ROW_CAP = 1408
LORA_ROW_CAP = 704
LN_ROW_CAP = 272


def _cparams(sem):
    return pltpu.CompilerParams(dimension_semantics=sem, vmem_limit_bytes=VMEM_LIMIT)


def _pick(n, prefs):
    for p in prefs:
        if n % p == 0:
            return p
    return n


def _row_tile(n, cap, align=BF16_ROWS):
    fits = [t for t in range(align, min(n, cap) + 1, align) if n % t == 0]
    return fits[-1] if fits else n


def _mm_kernel(a_ref, w_ref, *rest, nk, act):
    o_ref = rest[-1]
    res_ref = rest[0] if len(rest) == 2 else None
    if nk == 1:
        acc = jnp.dot(a_ref[...], w_ref[...].astype(BF16), preferred_element_type=F32)
        if act == "relu2":
            acc = jnp.square(jnp.maximum(acc, 0.0))
        if res_ref is not None:
            acc = DEEPNORM_ALPHA * res_ref[...] + acc
        o_ref[...] = acc.astype(o_ref.dtype)
        return

    tn = o_ref.shape[1]
    cw = _pick(tn, (MXU_COLS, LANES))

    def chunks(first):
        a = a_ref[...]
        for c in range(tn // cw):
            cols = slice(c * cw, (c + 1) * cw)
            part = jnp.dot(a, w_ref[:, cols].astype(BF16), preferred_element_type=F32)
            if not first:
                part = o_ref[:, cols] + part
            elif res_ref is not None:
                part = DEEPNORM_ALPHA * res_ref[:, cols] + part
            o_ref[:, cols] = part

    k = pl.program_id(2)
    pl.when(k == 0)(functools.partial(chunks, True))
    pl.when(k > 0)(functools.partial(chunks, False))


def _matmul(a, w, *, a_prefix=(), w_prefix=(), act="none", out_dtype=F32, residual=None,
            tm=None, tn=None, tk=None):
    M, K = a.shape[-2:]
    N = w.shape[-1]
    tm = tm or _row_tile(M, MM_ROW_CAP)
    tn = _pick(N, (tn or 512, 512, 256, 128))
    tk = _pick(K, (tk or 4096, 2048, 1024, 512, 256, 128))
    nk = K // tk
    na, nw = len(a_prefix), len(w_prefix)
    a_spec = pl.BlockSpec((None,) * na + (tm, tk), lambda i, j, k: a_prefix + (i, k))
    w_spec = pl.BlockSpec((None,) * nw + (tk, tn), lambda i, j, k: w_prefix + (k, j))
    assert nk == 1 or (act == "none" and out_dtype == F32), "K-tiled path accumulates in the f32 output"
    o_spec = pl.BlockSpec((tm, tn), lambda i, j, k: (i, j))
    extra = () if residual is None else (residual,)
    return pl.pallas_call(
        functools.partial(_mm_kernel, nk=nk, act=act),
        out_shape=jax.ShapeDtypeStruct((M, N), out_dtype),
        grid=(M // tm, N // tn, nk),
        in_specs=[a_spec, w_spec] + [o_spec] * len(extra),
        out_specs=o_spec,
        compiler_params=_cparams(("parallel", "parallel", "arbitrary")),
        name="matmul_" + act + ("_res" if extra else ""),
    )(a, w, *extra)


def _lora_up_kernel(t_ref, w2_ref, o_ref):
    o_ref[...] = jnp.dot(t_ref[...].astype(BF16), w2_ref[...], preferred_element_type=F32)


def _lora_up(t, w2):
    M, R = t.shape
    N = w2.shape[-1]
    tm = _row_tile(M, LORA_ROW_CAP)
    return pl.pallas_call(
        _lora_up_kernel,
        out_shape=jax.ShapeDtypeStruct((M, N), F32),
        grid=(M // tm,),
        in_specs=[pl.BlockSpec((tm, R), lambda i: (i, 0)), pl.BlockSpec((R, N), lambda i: (0, 0))],
        out_specs=pl.BlockSpec((tm, N), lambda i: (i, 0)),
        compiler_params=_cparams(("parallel",)),
        name="lora_up",
    )(t, w2)


def _mix_kernel(x_ref, xp_ref, mu_ref, w1_ref, a1_ref, g1_ref, o_ref, tw_ref, ta_ref, tg_ref):
    x = x_ref[0]
    tt = x.shape[0]
    first = pl.program_id(1) == 0
    prev_row = jnp.where(first, 0.0, xp_ref[0, SUBLANES - 1:SUBLANES, :])
    rolled = pltpu.roll(x, 1, 0)
    row = lax.broadcasted_iota(jnp.int32, (tt, 1), 0)
    xprev = jnp.where(row == 0, prev_row, rolled)
    xx = xprev - x
    n_out = o_ref.shape[0]
    for n in range(n_out):
        o_ref[n, 0] = (x + xx * mu_ref[n:n + 1, :]).astype(o_ref.dtype)

    d = pl.program_id(2)

    @pl.when(d == 0)
    def _():
        for t_ref in (tw_ref, ta_ref, tg_ref):
            t_ref[...] = jnp.zeros_like(t_ref)

    for n, (w_ref, t_ref) in enumerate(((w1_ref, tw_ref), (a1_ref, ta_ref), (g1_ref, tg_ref))):
        mixed = (x + xx * mu_ref[n_out + n:n_out + n + 1, :]).astype(BF16)
        t_ref[0] += jnp.dot(mixed, w_ref[...], preferred_element_type=F32)

    @pl.when(d == pl.num_programs(2) - 1)
    def _():
        tw_ref[0] = jnp.tanh(tw_ref[0])
        tg_ref[0] = 1.0 / (1.0 + jnp.exp(-tg_ref[0]))


def _rwkv_mix(h, mu, w1, a1, g1):
    B, T, D = h.shape
    n_mu = mu.shape[0]
    n_out = n_mu - 3
    tt = _row_tile(T, LORA_ROW_CAP)
    dc = _pick(D, (2048, 1024, 512, 256, 128))
    sub = tt // SUBLANES
    lora_in = lambda w: pl.BlockSpec((dc, w.shape[-1]), lambda b, i, d: (d, 0))
    lora_out = lambda w: pl.BlockSpec((1, tt, w.shape[-1]), lambda b, i, d: (b, i, 0))
    return pl.pallas_call(
        _mix_kernel,
        out_shape=(jax.ShapeDtypeStruct((n_out, B, T, D), BF16),)
        + tuple(jax.ShapeDtypeStruct((B, T, w.shape[-1]), F32) for w in (w1, a1, g1)),
        grid=(B, T // tt, D // dc),
        in_specs=[
            pl.BlockSpec((1, tt, dc), lambda b, i, d: (b, i, d)),
            pl.BlockSpec((1, SUBLANES, dc), lambda b, i, d: (b, jnp.maximum(i * sub - 1, 0), d)),
            pl.BlockSpec((n_mu, dc), lambda b, i, d: (0, d)),
            lora_in(w1), lora_in(a1), lora_in(g1),
        ],
        out_specs=(pl.BlockSpec((n_out, 1, tt, dc), lambda b, i, d: (0, b, i, d)),
                   lora_out(w1), lora_out(a1), lora_out(g1)),
        compiler_params=_cparams(("parallel", "parallel", "arbitrary")),
        name="rwkv_mix",
    )(h, h, mu, w1, a1, g1)


def _ln_kernel(z_ref, g_ref, b_ref, o_ref, ob_ref):
    z = z_ref[...]
    mu = jnp.mean(z, axis=-1, keepdims=True)
    zc = z - mu
    var = jnp.mean(jnp.square(zc), axis=-1, keepdims=True)
    out = zc * lax.rsqrt(var + LN_EPS) * g_ref[...] + b_ref[...]
    o_ref[...] = out
    ob_ref[...] = out.astype(BF16)


def _layernorm(z, g, b):
    M, D = z.shape
    tm = _row_tile(M, LN_ROW_CAP)
    row = pl.BlockSpec((tm, D), lambda i: (i, 0))
    vec = pl.BlockSpec((1, D), lambda i: (0, 0))
    return pl.pallas_call(
        _ln_kernel,
        out_shape=(jax.ShapeDtypeStruct((M, D), F32), jax.ShapeDtypeStruct((M, D), BF16)),
        grid=(M // tm,),
        in_specs=[row, vec, vec],
        out_specs=(row, row),
        compiler_params=_cparams(("parallel",)),
        name="layernorm",
    )(z, g.reshape(1, D), b.reshape(1, D))


def _ln_tail_kernel(z_ref, g_ref, b_ref, o_ref):
    z = z_ref[0]
    mu = jnp.mean(z, axis=-1, keepdims=True)
    zc = z - mu
    var = jnp.mean(jnp.square(zc), axis=-1, keepdims=True)
    o_ref[...] = zc * lax.rsqrt(var + LN_EPS) * g_ref[...] + b_ref[...]


def _layernorm_drop_prefix(z, g, b, skip):
    B, T, D = z.shape
    assert skip % SUBLANES == 0, "row offset must stay sublane-aligned"
    tt = _row_tile(T - skip, LN_ROW_CAP, align=SUBLANES)
    vec = pl.BlockSpec((1, D), lambda bi, i: (0, 0))
    return pl.pallas_call(
        _ln_tail_kernel,
        out_shape=jax.ShapeDtypeStruct((B, T - skip, D), F32),
        grid=(B, (T - skip) // tt),
        in_specs=[pl.BlockSpec((pl.Element(1), pl.Element(tt), pl.Element(D)),
                               lambda bi, i: (bi, pl.multiple_of(skip + i * tt, SUBLANES), 0)), vec, vec],
        out_specs=pl.BlockSpec((None, tt, D), lambda bi, i: (bi, i, 0)),
        compiler_params=_cparams(("parallel", "parallel")),
        name="layernorm_out",
    )(z, g.reshape(1, D), b.reshape(1, D))


def _split3(x):
    hi = x.astype(BF16)
    r1 = x - hi.astype(F32)
    mid = r1.astype(BF16)
    lo = (r1 - mid.astype(F32)).astype(BF16)
    return hi, mid, lo


def _split2(x):
    hi = x.astype(BF16)
    lo = (x - hi.astype(F32)).astype(BF16)
    return hi, lo


def _dot(a, b):
    return jnp.dot(a, b, preferred_element_type=F32)


def _dot_nt(a, b):
    return lax.dot_general(a, b, (((1,), (1,)), ((), ())), preferred_element_type=F32)


def _dot_tn(a, b):
    return lax.dot_general(a, b, (((0,), (0,)), ((), ())), preferred_element_type=F32)


def _scan_kernel(r_ref, k_ref, v_ref, wl_ref, al_ref, g_ref,
                 w0_ref, a0_ref, kk_ref, ka_ref, rk_ref, gg_ref, gb_ref,
                 o_ref, s_ref, *, seq_len):
    L = CHUNK
    hb = r_ref.shape[-1]
    n_pairs = hb // LANES
    P = 2 * L

    @pl.when(pl.program_id(2) == 0)
    def _():
        s_ref[...] = jnp.zeros_like(s_ref)

    if seq_len % L:
        row = pl.program_id(2) * L + lax.broadcasted_iota(jnp.int32, (L, 1), 0)
        load = lambda ref: jnp.where(row < seq_len, ref[0], 0.0)
    else:
        load = lambda ref: ref[0]
    r = load(r_ref)
    k = load(k_ref)
    v = load(v_ref)

    wpre = w0_ref[...] + load(wl_ref)
    nsp = jnp.maximum(-wpre, 0.0) + jnp.log1p(jnp.exp(-jnp.abs(wpre)))
    lw = -jnp.exp(-nsp - 0.5)
    a = 1.0 / (1.0 + jnp.exp(-(a0_ref[...] + load(al_ref))))

    li = lax.broadcasted_iota(jnp.int32, (2 * LANES, LANES), 0)
    lj = lax.broadcasted_iota(jnp.int32, (2 * LANES, LANES), 1)
    seg_ones = jnp.where(((li % LANES) // RWKV_HEAD) == (lj // RWKV_HEAD), 1.0, 0.0).astype(BF16)
    ti = lax.broadcasted_iota(jnp.int32, (L, L), 0)
    tj = lax.broadcasted_iota(jnp.int32, (L, L), 1)
    tril_incl = jnp.where(tj <= ti, 1.0, 0.0).astype(BF16)

    def to_rows(x):
        return jnp.concatenate([x[:, p * LANES:(p + 1) * LANES] for p in range(n_pairs)], axis=0)

    def to_lanes(x):
        return jnp.concatenate([x[p * L:(p + 1) * L] for p in range(n_pairs)], axis=1)

    def segsum(x):
        hi, lo = _split2(to_rows(x))
        return to_lanes(_dot(jnp.concatenate([hi, lo], axis=1), seg_ones))

    parts = _split3(lw)
    cs = _dot(tril_incl, jnp.concatenate(parts, axis=1))
    G = cs[:, :hb] + cs[:, hb:2 * hb] + cs[:, 2 * hb:]
    GL = G[L - 1:L, :]
    enG = jnp.exp(-G)
    eGL = jnp.exp(GL)
    eTail = jnp.exp(GL - G)

    kkr = k * kk_ref[...]
    kk = kkr / jnp.maximum(jnp.sqrt(segsum(kkr * kkr)), L2_EPS)
    k2 = k * (1.0 + (a - 1.0) * ka_ref[...])
    b = kk * a
    full = dict(R=r * jnp.exp(G), Q=kk * jnp.exp(G - lw), Kt=k2 * enG, Bt=b * enG,
                Kb=k2 * eTail, Bb=b * eTail, V=v)

    pi = lax.broadcasted_iota(jnp.int32, (P, P), 0)
    pj = lax.broadcasted_iota(jnp.int32, (P, P), 1)
    strict = (pj % L) < (pi % L)
    incl = (pj % L) <= (pi % L)
    eye = jnp.where(pi == pj, 1.0, 0.0)
    lane = lax.broadcasted_iota(jnp.int32, (L, LANES), 1)
    head0 = lane < RWKV_HEAD

    def stack(x):
        return jnp.concatenate([jnp.where(head0, x, 0.0), jnp.where(head0, 0.0, x)], axis=0).astype(BF16)

    pairs = range(n_pairs)
    st = {name: [stack(x[:, p * LANES:(p + 1) * LANES]) for p in pairs] for name, x in full.items()}
    S = [s_ref[p] for p in pairs]

    AA = [_dot_nt(jnp.concatenate([st["Q"][p], st["R"][p]], axis=0),
                  jnp.concatenate([st["Bt"][p], st["Kt"][p]], axis=0)) for p in pairs]
    A_qk = [jnp.where(strict, AA[p][:P, P:], 0.0).astype(BF16) for p in pairs]
    A_r = [jnp.concatenate([jnp.where(incl, AA[p][P:, P:], 0.0).astype(BF16),
                            jnp.where(incl, -AA[p][P:, :P], 0.0).astype(BF16)], axis=1) for p in pairs]
    AV = [_dot(A_qk[p], st["V"][p]) for p in pairs]

    Npow = [jnp.where(strict, -AA[p][:P, :P], 0.0) for p in pairs]
    T = [eye + Npow[p] for p in pairs]
    for _ in range((L - 1).bit_length() - 1):
        nb = [Npow[p].astype(BF16) for p in pairs]
        Npow = [_dot(nb[p], nb[p]) for p in pairs]
        T = [T[p] + _dot(T[p].astype(BF16), Npow[p].astype(BF16)) for p in pairs]

    WU = [_dot(T[p].astype(BF16), jnp.concatenate([st["Q"][p], AV[p].astype(BF16)], axis=1)) for p in pairs]
    WR = [_dot_nt(jnp.concatenate([WU[p][:, :LANES].astype(BF16), st["R"][p]], axis=0), S[p].astype(BF16))
          for p in pairs]
    Ub = [(WR[p][:P] + WU[p][:, LANES:]).astype(BF16) for p in pairs]
    Ys = [WR[p][P:] + _dot(A_r[p], jnp.concatenate([st["V"][p], Ub[p]], axis=0)) for p in pairs]
    for p in pairs:
        s_ref[p] = S[p] * eGL[:, p * LANES:(p + 1) * LANES] + _dot_tn(
            jnp.concatenate([st["V"][p], -Ub[p]], axis=0), jnp.concatenate([st["Kb"][p], st["Bb"][p]], axis=0))
    y = jnp.concatenate([Ys[p][:L] + Ys[p][L:] for p in pairs], axis=1)

    mu = segsum(y) * (1.0 / RWKV_HEAD)
    yc = y - mu
    var = segsum(yc * yc) * (1.0 / RWKV_HEAD)
    y_n = yc * lax.rsqrt(var + GN_EPS) * gg_ref[...] + gb_ref[...]
    bonus = segsum(r * k2 * rk_ref[...]) * v
    o_ref[0] = ((y_n + bonus) * load(g_ref)).astype(o_ref.dtype)


def _rwkv_scan(r, k, v, wl, al, g, w0, a0, k_k, k_a, r_k, gn_g, gn_b):
    B, T, D = r.shape
    hb = _pick(D, (4096, 2048, 1024, 512, 256, 128))
    seq = pl.BlockSpec((1, CHUNK, hb), lambda b, h, c: (b, c, h))
    vec = pl.BlockSpec((1, hb), lambda b, h, c: (0, h))
    vecs = [x.reshape(1, D) for x in (w0, a0, k_k, k_a, r_k, gn_g, gn_b)]
    return pl.pallas_call(
        functools.partial(_scan_kernel, seq_len=T),
        out_shape=jax.ShapeDtypeStruct((B, T, D), BF16),
        grid=(B, D // hb, pl.cdiv(T, CHUNK)),
        in_specs=[seq] * 6 + [vec] * 7,
        out_specs=seq,
        scratch_shapes=[pltpu.VMEM((hb // LANES, LANES, LANES), F32)],
        compiler_params=_cparams(("parallel", "parallel", "arbitrary")),
        name="rwkv7_scan",
    )(r, k, v, wl, al, g, *vecs)


def _sb_kernel(q_ref, k_ref, v_ref, cm_ref, o_ref, c_ref, acc_ref, *, seq_len):
    tq = q_ref.shape[1]
    tk = SB_HEAD
    nbq = tq // tk
    n_h = q_ref.shape[2] // SB_HEAD
    heads = range(n_h)
    qi = pl.program_id(2)
    scale = SB_HEAD ** -0.5
    q = [q_ref[0, :, g * SB_HEAD:(g + 1) * SB_HEAD] for g in heads]
    cm = cm_ref[...]

    def step(kb, nb, diag, carry, tail=0):
        w = nb * tk
        if tail:
            def load(ref, cols):
                x = ref[0, kb * tk:kb * tk + tail, cols]
                return jnp.concatenate([x, jnp.zeros((tk - tail, SB_HEAD), x.dtype)], axis=0)
        else:
            rows = pl.ds(pl.multiple_of(kb * tk, tk), w)
            load = lambda ref, cols: ref[0, rows, cols]
        if diag:
            ri = lax.broadcasted_iota(jnp.int32, (tq, w), 0)
            ci = lax.broadcasted_iota(jnp.int32, (tq, w), 1)
            causal = ci < ri
        zn, lk = [], []
        for g in heads:
            ks = load(k_ref, slice(g * SB_HEAD, (g + 1) * SB_HEAD))
            zng = _dot_nt(q[g], ks) * (-scale)
            neg_abs = lax.bitcast_convert_type(
                lax.bitcast_convert_type(zng, jnp.int32) | jnp.int32(-2 ** 31), F32)
            lkg = jnp.minimum(zng, 0.0) - jnp.log(jnp.maximum(1.0 + jnp.exp(neg_abs), 1.0))
            if diag:
                lkg = jnp.where(causal, lkg, 0.0)
            zn.append(zng)
            lk.extend(lkg[:, j * tk:(j + 1) * tk] for j in range(nb))
        hi, lo = _split2(jnp.concatenate(lk, axis=0))
        sums = _dot(jnp.concatenate([hi, lo], axis=1), cm)
        out = []
        for g in heads:
            c, acc = carry[g]
            sg = [sums[(g * nb + j) * tq:(g * nb + j + 1) * tq] for j in range(nb)]
            cs = [c]
            for j in range(nb - 1, 0, -1):
                cs.insert(0, cs[0] + sg[j][:, tk:])
            logit = jnp.concatenate([sg[j][:, :tk] + cs[j] for j in range(nb)], axis=1) - zn[g]
            pexp = jnp.exp(logit)
            if diag:
                pexp = jnp.where(causal, pexp, 0.0)
            vs = load(v_ref, slice(g * SB_HEAD, (g + 1) * SB_HEAD))
            out.append((cs[0] + sg[0][:, tk:], acc + _dot(pexp.astype(BF16), vs)))
        return tuple(out)

    def save(carry):
        for g in heads:
            c_ref[g], acc_ref[g] = carry[g]

    def restore():
        return tuple((c_ref[g], acc_ref[g]) for g in heads)

    init = tuple((jnp.zeros((tq, tk), F32), jnp.zeros((tq, SB_HEAD), F32)) for _ in heads)
    diag_tile = lambda: save(step(qi * nbq, nbq, True, init))
    last, tail = divmod(seq_len, tq)
    if tail:
        assert tail <= tk, "ragged last query block must fit one key block"
        pl.when(qi == last)(lambda: save(step(last * nbq, 1, True, init, tail=tail)))
        pl.when(qi != last)(diag_tile)
    else:
        diag_tile()

    pl.when(qi % 2 == 1)(lambda: save(step((qi - 1) * nbq, nbq, False, restore())))
    n_trips = qi // 2

    def body(n, carry):
        t = 2 * (n_trips - 1 - n) + 1
        return step((t - 1) * nbq, nbq, False, step(t * nbq, nbq, False, carry))

    carry = lax.fori_loop(0, n_trips, body, restore())
    o_ref[0] = jnp.concatenate([carry[g][1] for g in heads], axis=1).astype(o_ref.dtype)


def _sb_attention(qkv, n_heads):
    B, T, _ = qkv.shape
    tk = SB_HEAD
    tq = 2 * tk
    gh = _pick(n_heads, (8, 4, 2, 1))
    ng = n_heads // gh
    ri = jnp.arange(2 * tk)[:, None] % tk
    ci = jnp.arange(2 * tk)[None, :]
    cm = jnp.where(ci < tk, ri >= ci, True).astype(BF16)
    return pl.pallas_call(
        functools.partial(_sb_kernel, seq_len=T),
        out_shape=jax.ShapeDtypeStruct((B, T, n_heads * SB_HEAD), BF16),
        grid=(B, ng, pl.cdiv(T, tq)),
        in_specs=[
            pl.BlockSpec((1, tq, gh * SB_HEAD), lambda b, h, i: (b, i, h)),
            pl.BlockSpec((1, T, gh * SB_HEAD), lambda b, h, i: (b, 0, ng + h)),
            pl.BlockSpec((1, T, gh * SB_HEAD), lambda b, h, i: (b, 0, 2 * ng + h)),
            pl.BlockSpec((2 * tk, 2 * tk), lambda b, h, i: (0, 0)),
        ],
        out_specs=pl.BlockSpec((1, tq, gh * SB_HEAD), lambda b, h, i: (b, i, h)),
        scratch_shapes=[pltpu.VMEM((gh, tq, tk), F32), pltpu.VMEM((gh, tq, SB_HEAD), F32)],
        compiler_params=_cparams(("parallel", "parallel", "arbitrary")),
        name="stick_breaking_attention",
    )(qkv, qkv, qkv, cm)


def kernel(x, meta_tokens, ln_mix_g, ln_mix_b, ln_ffn_g, ln_ffn_b, w_up, w_down, rwkv_mu, rwkv_w_rkv, rwkv_w0, rwkv_w1, rwkv_w2, rwkv_a0, rwkv_a1, rwkv_a2, rwkv_g1, rwkv_g2, rwkv_k_k, rwkv_k_a, rwkv_r_k, rwkv_gn_g, rwkv_gn_b, rwkv_w_o, sb_w_qkv, sb_w_o):
    B, seq_len, D = x.shape
    depth = w_up.shape[0]
    T = N_META + seq_len
    M = B * T
    meta = jnp.broadcast_to(meta_tokens[None].astype(x.dtype), (B, N_META, D))
    h = jnp.concatenate([meta, x], axis=1).reshape(M, D)
    hb = None
    for i in range(depth):
        j = i // 2
        if i % 2 == 0:
            mixed, tw, ta, tg = _rwkv_mix(h.reshape(B, T, D), rwkv_mu[j], rwkv_w1[j].astype(BF16),
                                          rwkv_a1[j].astype(BF16), rwkv_g1[j].astype(BF16))
            mixed = mixed.reshape(-1, M, D)
            r, k, v = (_matmul(mixed, rwkv_w_rkv, a_prefix=(n,), w_prefix=(j, n)) for n in range(3))
            wl, al, g = (_lora_up(t.reshape(M, -1), w2[j].astype(BF16))
                         for t, w2 in ((tw, rwkv_w2), (ta, rwkv_a2), (tg, rwkv_g2)))
            r, k, v, wl, al, g = (z.reshape(B, T, D) for z in (r, k, v, wl, al, g))
            y = _rwkv_scan(r, k, v, wl, al, g, rwkv_w0[j], rwkv_a0[j], rwkv_k_k[j], rwkv_k_a[j],
                           rwkv_r_k[j], rwkv_gn_g[j], rwkv_gn_b[j])
            z = _matmul(y.reshape(M, D), rwkv_w_o, w_prefix=(j,), residual=h, tk=2048, tn=1024)
        else:
            qkv = _matmul(hb, sb_w_qkv, w_prefix=(j,), out_dtype=BF16)
            o = _sb_attention(qkv.reshape(B, T, 3 * D), D // SB_HEAD)
            z = _matmul(o.reshape(M, D), sb_w_o, w_prefix=(j,), residual=h, tk=2048, tn=1024)
        h, hb = _layernorm(z, ln_mix_g[i], ln_mix_b[i])
        up = _matmul(hb, w_up, w_prefix=(i,), act="relu2", out_dtype=BF16)
        z = _matmul(up, w_down, w_prefix=(i,), residual=h, tk=2048, tn=1024)
        if i < depth - 1:
            h, hb = _layernorm(z, ln_ffn_g[i], ln_ffn_b[i])
    return _layernorm_drop_prefix(z.reshape(B, T, D), ln_ffn_g[depth - 1], ln_ffn_b[depth - 1], N_META)
```

```python
import functools

import jax
import jax.numpy as jnp
from jax import lax
from jax.experimental import pallas as pl
from jax.experimental.pallas import tpu as pltpu

F32 = jnp.float32
BF16 = jnp.bfloat16

N_META = 16
RWKV_HEAD = 64
SB_HEAD = 128
LN_EPS = 1e-5
GN_EPS = 64e-5
L2_EPS = 1e-12
DEPTH = 2
DEEPNORM_ALPHA = (2 * DEPTH) ** 0.25

LANES = 128
SUBLANES = 8
MXU_COLS = 256
CHUNK = 64
VMEM_LIMIT = 56 * 1024 * 1024
BF16_ROWS = 16

MM_ROW_CAP = 1408
LORA_ROW_CAP = 704
LN_ROW_CAP = 272


def _cparams(sem):
    return pltpu.CompilerParams(dimension_semantics=sem, vmem_limit_bytes=VMEM_LIMIT)


def _pick(n, prefs):
    for p in prefs:
        if n % p == 0:
            return p
    return n


def _row_tile(n, cap, align=BF16_ROWS):
    fits = [t for t in range(align, min(n, cap) + 1, align) if n % t == 0]
    return fits[-1] if fits else n


def _mm_kernel(a_ref, w_ref, *rest, nk, act):
    o_ref = rest[-1]
    res_ref = rest[0] if len(rest) == 2 else None
    if nk == 1:
        acc = jnp.dot(a_ref[...], w_ref[...].astype(BF16), preferred_element_type=F32)
        if act == "relu2":
            acc = jnp.square(jnp.maximum(acc, 0.0))
        if res_ref is not None:
            acc = DEEPNORM_ALPHA * res_ref[...] + acc
        o_ref[...] = acc.astype(o_ref.dtype)
        return

    tn = o_ref.shape[1]
    cw = _pick(tn, (MXU_COLS, LANES))

    def chunks(first):
        a = a_ref[...]
        for c in range(tn // cw):
            cols = slice(c * cw, (c + 1) * cw)
            part = jnp.dot(a, w_ref[:, cols].astype(BF16), preferred_element_type=F32)
            if not first:
                part = o_ref[:, cols] + part
            elif res_ref is not None:
                part = DEEPNORM_ALPHA * res_ref[:, cols] + part
            o_ref[:, cols] = part

    k = pl.program_id(2)
    pl.when(k == 0)(functools.partial(chunks, True))
    pl.when(k > 0)(functools.partial(chunks, False))


def _matmul(a, w, *, a_prefix=(), w_prefix=(), act="none", out_dtype=F32, residual=None,
            tm=None, tn=None, tk=None):
    M, K = a.shape[-2:]
    N = w.shape[-1]
    tm = tm or _row_tile(M, MM_ROW_CAP)
    tn = _pick(N, (tn or 512, 512, 256, 128))
    tk = _pick(K, (tk or 4096, 2048, 1024, 512, 256, 128))
    nk = K // tk
    na, nw = len(a_prefix), len(w_prefix)
    a_spec = pl.BlockSpec((None,) * na + (tm, tk), lambda i, j, k: a_prefix + (i, k))
    w_spec = pl.BlockSpec((None,) * nw + (tk, tn), lambda i, j, k: w_prefix + (k, j))
    assert nk == 1 or (act == "none" and out_dtype == F32), "K-tiled path accumulates in the f32 output"
    o_spec = pl.BlockSpec((tm, tn), lambda i, j, k: (i, j))
    extra = () if residual is None else (residual,)
    return pl.pallas_call(
        functools.partial(_mm_kernel, nk=nk, act=act),
        out_shape=jax.ShapeDtypeStruct((M, N), out_dtype),
        grid=(M // tm, N // tn, nk),
        in_specs=[a_spec, w_spec] + [o_spec] * len(extra),
        out_specs=o_spec,
        compiler_params=_cparams(("parallel", "parallel", "arbitrary")),
        name="matmul_" + act + ("_res" if extra else ""),
    )(a, w, *extra)


def _mix_kernel(x_ref, xp_ref, mu_ref, w1_ref, a1_ref, g1_ref, o_ref, tw_ref, ta_ref, tg_ref):
    x = x_ref[0]
    tt = x.shape[0]
    first = pl.program_id(1) == 0
    prev_row = jnp.where(first, 0.0, xp_ref[0, SUBLANES - 1:SUBLANES, :])
    rolled = pltpu.roll(x, 1, 0)
    row = lax.broadcasted_iota(jnp.int32, (tt, 1), 0)
    xprev = jnp.where(row == 0, prev_row, rolled)
    xx = xprev - x
    n_out = o_ref.shape[0]
    for n in range(n_out):
        o_ref[n, 0] = (x + xx * mu_ref[n:n + 1, :]).astype(o_ref.dtype)

    d = pl.program_id(2)

    @pl.when(d == 0)
    def _():
        for t_ref in (tw_ref, ta_ref, tg_ref):
            t_ref[...] = jnp.zeros_like(t_ref)

    for n, (w_ref, t_ref) in enumerate(((w1_ref, tw_ref), (a1_ref, ta_ref), (g1_ref, tg_ref))):
        mixed = (x + xx * mu_ref[n_out + n:n_out + n + 1, :]).astype(BF16)
        t_ref[0] += jnp.dot(mixed, w_ref[...], preferred_element_type=F32)

    @pl.when(d == pl.num_programs(2) - 1)
    def _():
        tw_ref[0] = jnp.tanh(tw_ref[0])
        tg_ref[0] = 1.0 / (1.0 + jnp.exp(-tg_ref[0]))


def _rwkv_mix(h, mu, w1, a1, g1):
    B, T, D = h.shape
    n_mu = mu.shape[0]
    n_out = n_mu - 3
    tt = _row_tile(T, LORA_ROW_CAP)
    dc = _pick(D, (2048, 1024, 512, 256, 128))
    sub = tt // SUBLANES
    lora_in = lambda w: pl.BlockSpec((dc, w.shape[-1]), lambda b, i, d: (d, 0))
    lora_out = lambda w: pl.BlockSpec((1, tt, w.shape[-1]), lambda b, i, d: (b, i, 0))
    return pl.pallas_call(
        _mix_kernel,
        out_shape=(jax.ShapeDtypeStruct((n_out, B, T, D), BF16),)
        + tuple(jax.ShapeDtypeStruct((B, T, w.shape[-1]), F32) for w in (w1, a1, g1)),
        grid=(B, T // tt, D // dc),
        in_specs=[
            pl.BlockSpec((1, tt, dc), lambda b, i, d: (b, i, d)),
            pl.BlockSpec((1, SUBLANES, dc), lambda b, i, d: (b, jnp.maximum(i * sub - 1, 0), d)),
            pl.BlockSpec((n_mu, dc), lambda b, i, d: (0, d)),
            lora_in(w1), lora_in(a1), lora_in(g1),
        ],
        out_specs=(pl.BlockSpec((n_out, 1, tt, dc), lambda b, i, d: (0, b, i, d)),
                   lora_out(w1), lora_out(a1), lora_out(g1)),
        compiler_params=_cparams(("parallel", "parallel", "arbitrary")),
        name="rwkv_mix",
    )(h, h, mu, w1, a1, g1)


def _ln_kernel(z_ref, g_ref, b_ref, o_ref, ob_ref):
    z = z_ref[...]
    mu = jnp.mean(z, axis=-1, keepdims=True)
    zc = z - mu
    var = jnp.mean(jnp.square(zc), axis=-1, keepdims=True)
    out = zc * lax.rsqrt(var + LN_EPS) * g_ref[...] + b_ref[...]
    o_ref[...] = out
    ob_ref[...] = out.astype(BF16)


def _layernorm(z, g, b):
    M, D = z.shape
    tm = _row_tile(M, LN_ROW_CAP)
    row = pl.BlockSpec((tm, D), lambda i: (i, 0))
    vec = pl.BlockSpec((1, D), lambda i: (0, 0))
    return pl.pallas_call(
        _ln_kernel,
        out_shape=(jax.ShapeDtypeStruct((M, D), F32), jax.ShapeDtypeStruct((M, D), BF16)),
        grid=(M // tm,),
        in_specs=[row, vec, vec],
        out_specs=(row, row),
        compiler_params=_cparams(("parallel",)),
        name="layernorm",
    )(z, g.reshape(1, D), b.reshape(1, D))


def _ln_tail_kernel(z_ref, g_ref, b_ref, o_ref):
    z = z_ref[0]
    mu = jnp.mean(z, axis=-1, keepdims=True)
    zc = z - mu
    var = jnp.mean(jnp.square(zc), axis=-1, keepdims=True)
    o_ref[...] = zc * lax.rsqrt(var + LN_EPS) * g_ref[...] + b_ref[...]


def _layernorm_drop_prefix(z, g, b, skip):
    B, T, D = z.shape
    assert skip % SUBLANES == 0, "row offset must stay sublane-aligned"
    tt = _row_tile(T - skip, LN_ROW_CAP, align=SUBLANES)
    vec = pl.BlockSpec((1, D), lambda bi, i: (0, 0))
    return pl.pallas_call(
        _ln_tail_kernel,
        out_shape=jax.ShapeDtypeStruct((B, T - skip, D), F32),
        grid=(B, (T - skip) // tt),
        in_specs=[pl.BlockSpec((pl.Element(1), pl.Element(tt), pl.Element(D)),
                               lambda bi, i: (bi, pl.multiple_of(skip + i * tt, SUBLANES), 0)), vec, vec],
        out_specs=pl.BlockSpec((None, tt, D), lambda bi, i: (bi, i, 0)),
        compiler_params=_cparams(("parallel", "parallel")),
        name="layernorm_out",
    )(z, g.reshape(1, D), b.reshape(1, D))


def _split3(x):
    hi = x.astype(BF16)
    r1 = x - hi.astype(F32)
    mid = r1.astype(BF16)
    lo = (r1 - mid.astype(F32)).astype(BF16)
    return hi, mid, lo


def _split2(x):
    hi = x.astype(BF16)
    lo = (x - hi.astype(F32)).astype(BF16)
    return hi, lo


def _dot(a, b):
    return jnp.dot(a, b, preferred_element_type=F32)


def _dot_nt(a, b):
    return lax.dot_general(a, b, (((1,), (1,)), ((), ())), preferred_element_type=F32)


def _dot_tn(a, b):
    return lax.dot_general(a, b, (((0,), (0,)), ((), ())), preferred_element_type=F32)


def _scan_kernel(r_ref, k_ref, v_ref, tw_ref, ta_ref, tg_ref, w2_ref, a2_ref, g2_ref,
                 w0_ref, a0_ref, kk_ref, ka_ref, rk_ref, gg_ref, gb_ref,
                 o_ref, s_ref, *, seq_len):
    L = CHUNK
    hb = r_ref.shape[-1]
    n_pairs = hb // LANES
    P = 2 * L

    @pl.when(pl.program_id(2) == 0)
    def _():
        s_ref[...] = jnp.zeros_like(s_ref)

    if seq_len % L:
        row = pl.program_id(2) * L + lax.broadcasted_iota(jnp.int32, (L, 1), 0)
        load = lambda ref: jnp.where(row < seq_len, ref[0], 0.0)
    else:
        load = lambda ref: ref[0]
    r = load(r_ref)
    k = load(k_ref)
    v = load(v_ref)

    def lora_up(t_ref, up_ref):
        return _dot(load(t_ref).astype(BF16), up_ref[...])

    wpre = w0_ref[...] + lora_up(tw_ref, w2_ref)
    nsp = jnp.maximum(-wpre, 0.0) + jnp.log1p(jnp.exp(-jnp.abs(wpre)))
    lw = -jnp.exp(-nsp - 0.5)
    a = 1.0 / (1.0 + jnp.exp(-(a0_ref[...] + lora_up(ta_ref, a2_ref))))

    li = lax.broadcasted_iota(jnp.int32, (2 * LANES, LANES), 0)
    lj = lax.broadcasted_iota(jnp.int32, (2 * LANES, LANES), 1)
    seg_ones = jnp.where(((li % LANES) // RWKV_HEAD) == (lj // RWKV_HEAD), 1.0, 0.0).astype(BF16)
    ti = lax.broadcasted_iota(jnp.int32, (L, L), 0)
    tj = lax.broadcasted_iota(jnp.int32, (L, L), 1)
    tril_incl = jnp.where(tj <= ti, 1.0, 0.0).astype(BF16)

    def to_rows(x):
        return jnp.concatenate([x[:, p * LANES:(p + 1) * LANES] for p in range(n_pairs)], axis=0)

    def to_lanes(x):
        return jnp.concatenate([x[p * L:(p + 1) * L] for p in range(n_pairs)], axis=1)

    def segsum(x):
        hi, lo = _split2(to_rows(x))
        return to_lanes(_dot(jnp.concatenate([hi, lo], axis=1), seg_ones))

    parts = _split3(lw)
    cs = _dot(tril_incl, jnp.concatenate(parts, axis=1))
    G = cs[:, :hb] + cs[:, hb:2 * hb] + cs[:, 2 * hb:]
    GL = G[L - 1:L, :]
    enG = jnp.exp(-G)
    eGL = jnp.exp(GL)
    eTail = jnp.exp(GL - G)

    kkr = k * kk_ref[...]
    kk = kkr / jnp.maximum(jnp.sqrt(segsum(kkr * kkr)), L2_EPS)
    k2 = k * (1.0 + (a - 1.0) * ka_ref[...])
    b = kk * a
    full = dict(R=r * jnp.exp(G), Q=kk * jnp.exp(G - lw), Kt=k2 * enG, Bt=b * enG,
                Kb=k2 * eTail, Bb=b * eTail, V=v)

    pi = lax.broadcasted_iota(jnp.int32, (P, P), 0)
    pj = lax.broadcasted_iota(jnp.int32, (P, P), 1)
    strict = (pj % L) < (pi % L)
    incl = (pj % L) <= (pi % L)
    eye = jnp.where(pi == pj, 1.0, 0.0)
    lane = lax.broadcasted_iota(jnp.int32, (L, LANES), 1)
    head0 = lane < RWKV_HEAD

    def stack(x):
        return jnp.concatenate([jnp.where(head0, x, 0.0), jnp.where(head0, 0.0, x)], axis=0).astype(BF16)

    pairs = range(n_pairs)
    st = {name: [stack(x[:, p * LANES:(p + 1) * LANES]) for p in pairs] for name, x in full.items()}
    S = [s_ref[p] for p in pairs]

    AA = [_dot_nt(jnp.concatenate([st["Q"][p], st["R"][p]], axis=0),
                  jnp.concatenate([st["Bt"][p], st["Kt"][p]], axis=0)) for p in pairs]
    A_qk = [jnp.where(strict, AA[p][:P, P:], 0.0).astype(BF16) for p in pairs]
    A_r = [jnp.concatenate([jnp.where(incl, AA[p][P:, P:], 0.0).astype(BF16),
                            jnp.where(incl, -AA[p][P:, :P], 0.0).astype(BF16)], axis=1) for p in pairs]
    AV = [_dot(A_qk[p], st["V"][p]) for p in pairs]

    Npow = [jnp.where(strict, -AA[p][:P, :P], 0.0) for p in pairs]
    T = [eye + Npow[p] for p in pairs]
    for _ in range((L - 1).bit_length() - 1):
        nb = [Npow[p].astype(BF16) for p in pairs]
        Npow = [_dot(nb[p], nb[p]) for p in pairs]
        T = [T[p] + _dot(T[p].astype(BF16), Npow[p].astype(BF16)) for p in pairs]

    WU = [_dot(T[p].astype(BF16), jnp.concatenate([st["Q"][p], AV[p].astype(BF16)], axis=1)) for p in pairs]
    WR = [_dot_nt(jnp.concatenate([WU[p][:, :LANES].astype(BF16), st["R"][p]], axis=0), S[p].astype(BF16))
          for p in pairs]
    Ub = [(WR[p][:P] + WU[p][:, LANES:]).astype(BF16) for p in pairs]
    Ys = [WR[p][P:] + _dot(A_r[p], jnp.concatenate([st["V"][p], Ub[p]], axis=0)) for p in pairs]
    for p in pairs:
        s_ref[p] = S[p] * eGL[:, p * LANES:(p + 1) * LANES] + _dot_tn(
            jnp.concatenate([st["V"][p], -Ub[p]], axis=0), jnp.concatenate([st["Kb"][p], st["Bb"][p]], axis=0))
    y = jnp.concatenate([Ys[p][:L] + Ys[p][L:] for p in pairs], axis=1)

    mu = segsum(y) * (1.0 / RWKV_HEAD)
    yc = y - mu
    var = segsum(yc * yc) * (1.0 / RWKV_HEAD)
    y_n = yc * lax.rsqrt(var + GN_EPS) * gg_ref[...] + gb_ref[...]
    bonus = segsum(r * k2 * rk_ref[...]) * v
    o_ref[0] = ((y_n + bonus) * lora_up(tg_ref, g2_ref)).astype(o_ref.dtype)


def _rwkv_scan(r, k, v, feats, ups, w0, a0, k_k, k_a, r_k, gn_g, gn_b):
    B, T, D = r.shape
    hb = _pick(D, (4096, 2048, 1024, 512, 256, 128))
    seq = pl.BlockSpec((1, CHUNK, hb), lambda b, h, c: (b, c, h))
    vec = pl.BlockSpec((1, hb), lambda b, h, c: (0, h))
    vecs = [x.reshape(1, D) for x in (w0, a0, k_k, k_a, r_k, gn_g, gn_b)]
    return pl.pallas_call(
        functools.partial(_scan_kernel, seq_len=T),
        out_shape=jax.ShapeDtypeStruct((B, T, D), BF16),
        grid=(B, D // hb, pl.cdiv(T, CHUNK)),
        in_specs=[seq] * 3
        + [pl.BlockSpec((1, CHUNK, t.shape[-1]), lambda b, h, c: (b, c, 0)) for t in feats]
        + [pl.BlockSpec((u.shape[0], hb), lambda b, h, c: (0, h)) for u in ups]
        + [vec] * 7,
        out_specs=seq,
        scratch_shapes=[pltpu.VMEM((hb // LANES, LANES, LANES), F32)],
        compiler_params=_cparams(("parallel", "parallel", "arbitrary")),
        name="rwkv7_scan",
    )(r, k, v, *feats, *ups, *vecs)


def _sb_kernel(q_ref, k_ref, v_ref, cm_ref, o_ref, c_ref, acc_ref, *, seq_len):
    tq = q_ref.shape[1]
    tk = SB_HEAD
    nbq = tq // tk
    n_h = q_ref.shape[2] // SB_HEAD
    heads = range(n_h)
    qi = pl.program_id(2)
    scale = SB_HEAD ** -0.5
    q = [q_ref[0, :, g * SB_HEAD:(g + 1) * SB_HEAD] for g in heads]
    cm = cm_ref[...]

    def step(kb, nb, diag, carry, tail=0):
        w = nb * tk
        if tail:
            def load(ref, cols):
                x = ref[0, kb * tk:kb * tk + tail, cols]
                return jnp.concatenate([x, jnp.zeros((tk - tail, SB_HEAD), x.dtype)], axis=0)
        else:
            rows = pl.ds(pl.multiple_of(kb * tk, tk), w)
            load = lambda ref, cols: ref[0, rows, cols]
        if diag:
            ri = lax.broadcasted_iota(jnp.int32, (tq, w), 0)
            ci = lax.broadcasted_iota(jnp.int32, (tq, w), 1)
            causal = ci < ri
        zn, lk = [], []
        for g in heads:
            ks = load(k_ref, slice(g * SB_HEAD, (g + 1) * SB_HEAD))
            zng = _dot_nt(q[g], ks) * (-scale)
            neg_abs = lax.bitcast_convert_type(
                lax.bitcast_convert_type(zng, jnp.int32) | jnp.int32(-2 ** 31), F32)
            lkg = jnp.minimum(zng, 0.0) - jnp.log(jnp.maximum(1.0 + jnp.exp(neg_abs), 1.0))
            if diag:
                lkg = jnp.where(causal, lkg, 0.0)
            zn.append(zng)
            lk.extend(lkg[:, j * tk:(j + 1) * tk] for j in range(nb))
        hi, lo = _split2(jnp.concatenate(lk, axis=0))
        sums = _dot(jnp.concatenate([hi, lo], axis=1), cm)
        out = []
        for g in heads:
            c, acc = carry[g]
            sg = [sums[(g * nb + j) * tq:(g * nb + j + 1) * tq] for j in range(nb)]
            cs = [c]
            for j in range(nb - 1, 0, -1):
                cs.insert(0, cs[0] + sg[j][:, tk:])
            logit = jnp.concatenate([sg[j][:, :tk] + cs[j] for j in range(nb)], axis=1) - zn[g]
            pexp = jnp.exp(logit)
            if diag:
                pexp = jnp.where(causal, pexp, 0.0)
            vs = load(v_ref, slice(g * SB_HEAD, (g + 1) * SB_HEAD))
            out.append((cs[0] + sg[0][:, tk:], acc + _dot(pexp.astype(BF16), vs)))
        return tuple(out)

    def save(carry):
        for g in heads:
            c_ref[g], acc_ref[g] = carry[g]

    def restore():
        return tuple((c_ref[g], acc_ref[g]) for g in heads)

    init = tuple((jnp.zeros((tq, tk), F32), jnp.zeros((tq, SB_HEAD), F32)) for _ in heads)
    diag_tile = lambda: save(step(qi * nbq, nbq, True, init))
    last, tail = divmod(seq_len, tq)
    if tail:
        assert tail <= tk, "ragged last query block must fit one key block"
        pl.when(qi == last)(lambda: save(step(last * nbq, 1, True, init, tail=tail)))
        pl.when(qi != last)(diag_tile)
    else:
        diag_tile()

    pl.when(qi % 2 == 1)(lambda: save(step((qi - 1) * nbq, nbq, False, restore())))
    n_trips = qi // 2

    def body(n, carry):
        t = 2 * (n_trips - 1 - n) + 1
        return step((t - 1) * nbq, nbq, False, step(t * nbq, nbq, False, carry))

    carry = lax.fori_loop(0, n_trips, body, restore())
    o_ref[0] = jnp.concatenate([carry[g][1] for g in heads], axis=1).astype(o_ref.dtype)


def _sb_attention(qkv, n_heads):
    B, T, _ = qkv.shape
    tk = SB_HEAD
    tq = 2 * tk
    gh = _pick(n_heads, (8, 4, 2, 1))
    ng = n_heads // gh
    ri = jnp.arange(2 * tk)[:, None] % tk
    ci = jnp.arange(2 * tk)[None, :]
    cm = jnp.where(ci < tk, ri >= ci, True).astype(BF16)
    return pl.pallas_call(
        functools.partial(_sb_kernel, seq_len=T),
        out_shape=jax.ShapeDtypeStruct((B, T, n_heads * SB_HEAD), BF16),
        grid=(B, ng, pl.cdiv(T, tq)),
        in_specs=[
            pl.BlockSpec((1, tq, gh * SB_HEAD), lambda b, h, i: (b, i, h)),
            pl.BlockSpec((1, T, gh * SB_HEAD), lambda b, h, i: (b, 0, ng + h)),
            pl.BlockSpec((1, T, gh * SB_HEAD), lambda b, h, i: (b, 0, 2 * ng + h)),
            pl.BlockSpec((2 * tk, 2 * tk), lambda b, h, i: (0, 0)),
        ],
        out_specs=pl.BlockSpec((1, tq, gh * SB_HEAD), lambda b, h, i: (b, i, h)),
        scratch_shapes=[pltpu.VMEM((gh, tq, tk), F32), pltpu.VMEM((gh, tq, SB_HEAD), F32)],
        compiler_params=_cparams(("parallel", "parallel", "arbitrary")),
        name="stick_breaking_attention",
    )(qkv, qkv, qkv, cm)


def kernel(x, meta_tokens, ln_mix_g, ln_mix_b, ln_ffn_g, ln_ffn_b, w_up, w_down, rwkv_mu, rwkv_w_rkv, rwkv_w0, rwkv_w1, rwkv_w2, rwkv_a0, rwkv_a1, rwkv_a2, rwkv_g1, rwkv_g2, rwkv_k_k, rwkv_k_a, rwkv_r_k, rwkv_gn_g, rwkv_gn_b, rwkv_w_o, sb_w_qkv, sb_w_o):
    B, seq_len, D = x.shape
    depth = w_up.shape[0]
    T = N_META + seq_len
    M = B * T
    meta = jnp.broadcast_to(meta_tokens[None].astype(x.dtype), (B, N_META, D))
    h = jnp.concatenate([meta, x], axis=1).reshape(M, D)
    hb = None
    for i in range(depth):
        j = i // 2
        if i % 2 == 0:
            mixed, tw, ta, tg = _rwkv_mix(h.reshape(B, T, D), rwkv_mu[j], rwkv_w1[j].astype(BF16),
                                          rwkv_a1[j].astype(BF16), rwkv_g1[j].astype(BF16))
            mixed = mixed.reshape(-1, M, D)
            r, k, v = (_matmul(mixed, rwkv_w_rkv, a_prefix=(n,), w_prefix=(j, n)) for n in range(3))
            r, k, v = (z.reshape(B, T, D) for z in (r, k, v))
            ups = tuple(w2[j].astype(BF16) for w2 in (rwkv_w2, rwkv_a2, rwkv_g2))
            y = _rwkv_scan(r, k, v, (tw, ta, tg), ups, rwkv_w0[j], rwkv_a0[j], rwkv_k_k[j], rwkv_k_a[j],
                           rwkv_r_k[j], rwkv_gn_g[j], rwkv_gn_b[j])
            z = _matmul(y.reshape(M, D), rwkv_w_o, w_prefix=(j,), residual=h, tk=2048, tn=1024)
        else:
            qkv = _matmul(hb, sb_w_qkv, w_prefix=(j,), out_dtype=BF16)
            o = _sb_attention(qkv.reshape(B, T, 3 * D), D // SB_HEAD)
            z = _matmul(o.reshape(M, D), sb_w_o, w_prefix=(j,), residual=h, tk=2048, tn=1024)
        h, hb = _layernorm(z, ln_mix_g[i], ln_mix_b[i])
        up = _matmul(hb, w_up, w_prefix=(i,), act="relu2", out_dtype=BF16)
        z = _matmul(up, w_down, w_prefix=(i,), residual=h, tk=2048, tn=1024)
        if i < depth - 1:
            h, hb = _layernorm(z, ln_ffn_g[i], ln_ffn_b[i])
    return _layernorm_drop_prefix(z.reshape(B, T, D), ln_ffn_g[depth - 1], ln_ffn_b[depth - 1], N_META)
```
